```python
import jax, jax.numpy as jnp
from jax import lax
import numpy as np

D_MODEL = 1024
BATCH = 4
SEQ = 4096
DEPTH = 4
DEC_BATCH = 16
DEC_SEQ = 16
PAST_LEN = 1024

CHUNK = 64
D_MIX = D_MODEL
POOL_WIDTH = D_MIX // 2
POOL_WINDOWS = (2, 4, 8, 16)
N_POOL_GROUPS = len(POOL_WINDOWS)
POOL_GROUP = POOL_WIDTH // N_POOL_GROUPS
POOL_BUF = max(POOL_WINDOWS) - 1
GMLP_WIDTH = D_MIX - POOL_WIDTH
GMLP_HEADS = 4
GMLP_HEAD = GMLP_WIDTH // GMLP_HEADS
GMLP_LEN = 128
D_IN = POOL_WIDTH + 2 * GMLP_WIDTH
D_FF = ((8 * D_MODEL // 3 + 255) // 256) * 256
N_MOD = 6
EPS = 1e-6

kernel_name = "pool_gmlp_hybrid_stream_step"


def rms_norm(x, g):
    xf = x.astype(jnp.float32)
    y = xf * lax.rsqrt(jnp.mean(xf * xf, axis=-1, keepdims=True) + EPS)
    return (y * g.astype(jnp.float32)).astype(x.dtype)


def layer_norm(x, g, b):
    xf = x.astype(jnp.float32)
    mu = jnp.mean(xf, axis=-1, keepdims=True)
    var = jnp.mean(jnp.square(xf - mu), axis=-1, keepdims=True)
    y = (xf - mu) * lax.rsqrt(var + EPS)
    return (y * g.astype(jnp.float32) + b.astype(jnp.float32)).astype(x.dtype)


def pool_mixer(p, hist, pos0, w_pool, pool_scale):
    B, L, _ = p.shape
    full = jnp.concatenate([hist.astype(p.dtype), p], axis=1)
    cs = jnp.cumsum(full.astype(jnp.float32), axis=1)
    cs = jnp.pad(cs, ((0, 0), (1, 0), (0, 0)))
    end = POOL_BUF + 1
    pos = pos0 + jnp.arange(L, dtype=jnp.int32)
    outs = []
    for g, w in enumerate(POOL_WINDOWS):
        sl = slice(g * POOL_GROUP, (g + 1) * POOL_GROUP)
        win_sum = cs[:, end:end + L, sl] - cs[:, end - w:end - w + L, sl]
        count = jnp.minimum(pos + 1, w).astype(jnp.float32)[None, :, None]
        outs.append(win_sum / count - p[:, :, sl].astype(jnp.float32))
    d = jnp.stack(outs, axis=2).astype(p.dtype)
    y = jnp.einsum('blgc,gcd->blgd', d, w_pool).reshape(B, L, POOL_WIDTH) * pool_scale
    return y, full[:, -POOL_BUF:]


def gmlp_mixer(u, v, w_s, b_s):
    B, L, _ = v.shape
    Lc = min(L, GMLP_LEN)
    N = L // Lc
    idx = jnp.arange(GMLP_LEN) // CHUNK
    mask = idx[None, :] <= idx[:, None]
    wm = jnp.where(mask[None], w_s, jnp.zeros((), w_s.dtype))[:, :Lc, :Lc]
    vb = v.reshape(B, N, Lc, GMLP_HEADS, GMLP_HEAD)
    z = jnp.einsum('hts,bnshc->bnthc', wm.astype(v.dtype), vb)
    z = z + b_s[:, :Lc].T.astype(v.dtype)[None, None, :, :, None]
    return u * z.reshape(B, L, GMLP_WIDTH)


def trunk_layer(x, c, hist, pos0, w_ada, b_ada, g_mix, w_in, w_pool, pool_scale, ln_g, ln_b,
                w_s, b_s, w_out, g_ffn, w_gate, w_up, w_down):
    mod = jnp.dot(jax.nn.silu(c), w_ada) + b_ada
    sh_m, sc_m, gt_m, sh_f, sc_f, gt_f = [m[:, None, :] for m in jnp.split(mod, N_MOD, axis=-1)]
    h = rms_norm(x, g_mix) * (1 + sc_m) + sh_m
    z = jnp.dot(h, w_in)
    p = z[..., :POOL_WIDTH]
    u = jax.nn.gelu(z[..., POOL_WIDTH:POOL_WIDTH + GMLP_WIDTH])
    v = layer_norm(jax.nn.gelu(z[..., POOL_WIDTH + GMLP_WIDTH:]), ln_g, ln_b)
    y_pool, new_hist = pool_mixer(p, hist, pos0, w_pool, pool_scale)
    y_gmlp = gmlp_mixer(u, v, w_s, b_s)
    mix = jnp.dot(jnp.concatenate([y_pool, y_gmlp], axis=-1), w_out)
    x = x + gt_m * mix
    h = rms_norm(x, g_ffn) * (1 + sc_f) + sh_f
    f = jnp.dot(jax.nn.silu(jnp.dot(h, w_gate)) * jnp.dot(h, w_up), w_down)
    x = x + gt_f * f
    return x, new_hist, v


def setup_inputs(seed: int = 0) -> dict:
    key = jax.random.key(seed)
    ks = jax.random.split(key, 24)
    f32 = jnp.float32
    nrm = lambda k, s, sc: jax.random.normal(k, s, f32) * sc
    return {
        "x_prompt": nrm(ks[0], (BATCH, SEQ, D_MODEL), 1.0),
        "x_sample": nrm(ks[1], (DEC_BATCH, DEC_SEQ, D_MODEL), 1.0),
        "c_prompt": nrm(ks[2], (BATCH, D_MODEL), 1.0),
        "c_sample": nrm(ks[3], (DEC_BATCH, D_MODEL), 1.0),
        "cache_pool": nrm(ks[4], (DEPTH, DEC_BATCH, POOL_BUF, POOL_WIDTH), 1.0),
        "w_ada": nrm(ks[5], (DEPTH, D_MODEL, N_MOD * D_MODEL), 0.5 * D_MODEL ** -0.5),
        "b_ada": nrm(ks[6], (DEPTH, N_MOD * D_MODEL), 0.01),
        "g_mix": 1.0 + nrm(ks[7], (DEPTH, D_MODEL), 0.02),
        "w_in": nrm(ks[8], (DEPTH, D_MODEL, D_IN), D_MODEL ** -0.5),
        "w_pool": nrm(ks[9], (DEPTH, N_POOL_GROUPS, POOL_GROUP, POOL_GROUP), POOL_GROUP ** -0.5),
        "pool_scale": 1.0 + nrm(ks[10], (DEPTH, POOL_WIDTH), 0.02),
        "ln_g": 1.0 + nrm(ks[11], (DEPTH, GMLP_WIDTH), 0.02),
        "ln_b": nrm(ks[12], (DEPTH, GMLP_WIDTH), 0.02),
        "w_s": nrm(ks[13], (DEPTH, GMLP_HEADS, GMLP_LEN, GMLP_LEN), GMLP_LEN ** -0.5),
        "b_s": 1.0 + nrm(ks[14], (DEPTH, GMLP_HEADS, GMLP_LEN), 0.02),
        "w_out": nrm(ks[15], (DEPTH, D_MIX, D_MODEL), D_MIX ** -0.5),
        "g_ffn": 1.0 + nrm(ks[16], (DEPTH, D_MODEL), 0.02),
        "w_gate": nrm(ks[17], (DEPTH, D_MODEL, D_FF), D_MODEL ** -0.5),
        "w_up": nrm(ks[18], (DEPTH, D_MODEL, D_FF), D_MODEL ** -0.5),
        "w_down": nrm(ks[19], (DEPTH, D_FF, D_MODEL), D_FF ** -0.5),
        "g_final": 1.0 + nrm(ks[20], (D_MODEL,), 0.02),
    }


def reference(x_prompt, x_sample, c_prompt, c_sample, cache_pool, w_ada, b_ada, g_mix, w_in,
              w_pool, pool_scale, ln_g, ln_b, w_s, b_s, w_out, g_ffn, w_gate, w_up, w_down, g_final):
    xp, xs = x_prompt, x_sample
    hist_p0 = jnp.zeros((x_prompt.shape[0], POOL_BUF, POOL_WIDTH), x_prompt.dtype)
    pool_p, pool_s, v_s = [], [], []
    for l in range(DEPTH):
        lw = (w_ada[l], b_ada[l], g_mix[l], w_in[l], w_pool[l], pool_scale[l], ln_g[l], ln_b[l],
              w_s[l], b_s[l], w_out[l], g_ffn[l], w_gate[l], w_up[l], w_down[l])
        xp, hp, _ = trunk_layer(xp, c_prompt, hist_p0, 0, *lw)
        xs, hs, vs = trunk_layer(xs, c_sample, cache_pool[l], PAST_LEN, *lw)
        pool_p.append(hp)
        pool_s.append(hs)
        v_s.append(vs)
    y_prompt = rms_norm(xp, g_final)
    y_sample = rms_norm(xs, g_final)
    state_pool_prompt = jnp.stack(pool_p)
    state_pool_sample = jnp.stack(pool_s)
    state_gmlp_v_sample = jnp.stack(v_s)
    return (y_prompt, y_sample, state_pool_prompt, state_pool_sample, state_gmlp_v_sample)
```

```python
import functools

import jax
import jax.numpy as jnp
from jax import lax
from jax.experimental import pallas as pl
from jax.experimental.pallas import tpu as pltpu

D_MODEL = 1024
POOL_WIDTH = 512
POOL_WINDOWS = (2, 4, 8, 16)
POOL_GROUP = 128
HIST = 16
GMLP_WIDTH = 512
GMLP_HEADS = 4
GMLP_HEAD = 128
GMLP_LEN = 128
CHUNK = 64
D_IN = 1536
D_FF = 2816
N_MOD = 6
EPS = 1e-6

TM = 512
ADA_TN = 2048
VMEM_LIMIT = 56 * 1024 * 1024

BF16 = jnp.bfloat16
F32 = jnp.float32


def _dot(a, b):
    return jnp.dot(a, b, preferred_element_type=F32)


def _rms_norm(x, g):
    return x * lax.rsqrt(jnp.mean(x * x, axis=-1, keepdims=True) + EPS) * g


def _layer_norm(x, g, b):
    mu = jnp.mean(x, axis=-1, keepdims=True)
    xc = x - mu
    var = jnp.mean(xc * xc, axis=-1, keepdims=True)
    return xc * lax.rsqrt(var + EPS) * g + b


def _ada_kernel(c_ref, w_ref, b_ref, o_ref):
    s = jax.nn.silu(c_ref[...]).astype(BF16)
    o_ref[...] = _dot(s, w_ref[...].astype(BF16)) + b_ref[...]


def _ada_call(c_all, w_ada, b_ada):
    depth = w_ada.shape[0]
    rows = c_all.shape[0]
    n_out = w_ada.shape[2]
    return pl.pallas_call(
        _ada_kernel,
        grid=(depth, n_out // ADA_TN),
        in_specs=[
            pl.BlockSpec((rows, D_MODEL), lambda l, j: (0, 0)),
            pl.BlockSpec((None, D_MODEL, ADA_TN), lambda l, j: (l, 0, j)),
            pl.BlockSpec((None, 1, ADA_TN), lambda l, j: (l, 0, j)),
        ],
        out_specs=pl.BlockSpec((None, rows, ADA_TN), lambda l, j: (l, 0, j)),
        out_shape=jax.ShapeDtypeStruct((depth, rows, n_out), F32),
        compiler_params=pltpu.CompilerParams(
            dimension_semantics=("arbitrary", "arbitrary"), vmem_limit_bytes=VMEM_LIMIT),
        name="ada_mod",
    )(c_all, w_ada, b_ada.reshape(depth, 1, n_out))


def _mixer_in(x, sh, sc, g_mix, w_in_ref, ln_g, ln_b):
    h = (_rms_norm(x, g_mix) * (1.0 + sc) + sh).astype(BF16)
    z = _dot(h, w_in_ref[...])
    p = z[:, :POOL_WIDTH]
    u = jax.nn.gelu(z[:, POOL_WIDTH:POOL_WIDTH + GMLP_WIDTH])
    v = _layer_norm(jax.nn.gelu(z[:, POOL_WIDTH + GMLP_WIDTH:]), ln_g, ln_b)
    return p, u, v


def _ffn(x1, sh, sc, gt, g_ffn, w_gate_ref, w_up_ref, w_down_ref):
    h = (_rms_norm(x1, g_ffn) * (1.0 + sc) + sh).astype(BF16)
    a = (jax.nn.silu(_dot(h, w_gate_ref[...])) * _dot(h, w_up_ref[...])).astype(BF16)
    return x1 + gt * _dot(a, w_down_ref[...])


def _prompt_kernel(x_ref, mod_ref, g_mix_ref, w_in_ref, w_pool_ref, ps_ref, ln_g_ref, ln_b_ref,
                   w_s_ref, bs_ref, w_out_ref, g_ffn_ref, w_gate_ref, w_up_ref, w_down_ref,
                   g_fin_ref, o_ref, hist_ref, pbuf_ref, ybuf_ref, *, final):
    i = pl.program_id(1)
    tm = x_ref.shape[0]
    x = x_ref[...]
    sh_m, sc_m, gt_m = mod_ref[0:1, :], mod_ref[1:2, :], mod_ref[2:3, :]
    sh_f, sc_f, gt_f = mod_ref[3:4, :], mod_ref[4:5, :], mod_ref[5:6, :]

    p, u, v = _mixer_in(x, sh_m, sc_m, g_mix_ref[...], w_in_ref, ln_g_ref[...], ln_b_ref[...])

    @pl.when(i == 0)
    def _():
        pbuf_ref[0:HIST, :] = jnp.zeros((HIST, POOL_WIDTH), F32)

    pbuf_ref[HIST:HIST + tm, :] = p
    row = lax.broadcasted_iota(jnp.int32, (HIST, POOL_GROUP), 0) + i * tm
    for g, w in enumerate(POOL_WINDOWS):
        cols = pl.ds(g * POOL_GROUP, POOL_GROUP)
        acc = pbuf_ref[pl.ds(HIST, tm), cols]
        for j in range(1, w):
            acc = acc + pbuf_ref[pl.ds(HIST - j, tm), cols]
        pg = p[:, g * POOL_GROUP:(g + 1) * POOL_GROUP]
        cnt = jnp.minimum(row + 1, w).astype(F32)
        d = jnp.concatenate([acc[:HIST] / cnt, acc[HIST:] * (1.0 / w)], axis=0) - pg
        y = _dot(d.astype(BF16), w_pool_ref[g]) * ps_ref[:, g * POOL_GROUP:(g + 1) * POOL_GROUP]
        ybuf_ref[:, g * POOL_GROUP:(g + 1) * POOL_GROUP] = y.astype(BF16)
    last = p[tm - HIST:, :]
    pbuf_ref[0:HIST, :] = last
    hist_ref[...] = last

    r_idx = lax.broadcasted_iota(jnp.int32, (GMLP_LEN, GMLP_LEN), 0) // CHUNK
    c_idx = lax.broadcasted_iota(jnp.int32, (GMLP_LEN, GMLP_LEN), 1) // CHUNK
    vb = v.astype(BF16)
    for hd in range(GMLP_HEADS):
        wm = jnp.where(c_idx <= r_idx, w_s_ref[hd], 0.0).astype(BF16)
        hc = slice(hd * GMLP_HEAD, (hd + 1) * GMLP_HEAD)
        bias = bs_ref[:, hc]
        for c in range(tm // GMLP_LEN):
            rc = slice(c * GMLP_LEN, (c + 1) * GMLP_LEN)
            zc = _dot(wm, vb[rc, hc]) + bias
            ybuf_ref[rc, POOL_WIDTH + hd * GMLP_HEAD:POOL_WIDTH + (hd + 1) * GMLP_HEAD] = (
                u[rc, hc] * zc).astype(BF16)

    x1 = x + gt_m * _dot(ybuf_ref[...], w_out_ref[...])
    x2 = _ffn(x1, sh_f, sc_f, gt_f, g_ffn_ref[...], w_gate_ref, w_up_ref, w_down_ref)
    if final:
        x2 = _rms_norm(x2, g_fin_ref[...])
    o_ref[...] = x2


def _resident(shape, index_map):
    return pl.BlockSpec(shape, index_map, pipeline_mode=pl.Buffered(1))


def _prompt_call(layer, final, x, mod4, n_sample, vecs, mats):
    g_mix, ps, ln_g, ln_b, g_ffn, g_fin, bs_full = vecs
    w_in, w_pool, w_s, w_out, w_gate, w_up, w_down = mats
    batch, seq, _ = x.shape
    l = layer
    vec = lambda n: _resident((None, 1, n), lambda b, i: (l, 0, 0))
    in_specs = [
        pl.BlockSpec((None, TM, D_MODEL), lambda b, i: (b, i, 0)),
        pl.BlockSpec((None, None, N_MOD, D_MODEL), lambda b, i: (l, n_sample + b, 0, 0)),
        vec(D_MODEL),
        _resident((None, D_MODEL, D_IN), lambda b, i: (l, 0, 0)),
        _resident((None, len(POOL_WINDOWS), POOL_GROUP, POOL_GROUP), lambda b, i: (l, 0, 0, 0)),
        vec(POOL_WIDTH), vec(GMLP_WIDTH), vec(GMLP_WIDTH),
        _resident((None, GMLP_HEADS, GMLP_LEN, GMLP_LEN), lambda b, i: (l, 0, 0, 0)),
        _resident((None, GMLP_LEN, GMLP_WIDTH), lambda b, i: (l, 0, 0)),
        _resident((None, D_MODEL, D_MODEL), lambda b, i: (l, 0, 0)),
        vec(D_MODEL),
        _resident((None, D_MODEL, D_FF), lambda b, i: (l, 0, 0)),
        _resident((None, D_MODEL, D_FF), lambda b, i: (l, 0, 0)),
        _resident((None, D_FF, D_MODEL), lambda b, i: (l, 0, 0)),
        _resident((1, D_MODEL), lambda b, i: (0, 0)),
    ]
    out_specs = [
        pl.BlockSpec((None, TM, D_MODEL), lambda b, i: (b, i, 0)),
        pl.BlockSpec((None, HIST, POOL_WIDTH), lambda b, i: (b, 0, 0)),
    ]
    return pl.pallas_call(
        functools.partial(_prompt_kernel, final=final),
        grid=(batch, seq // TM),
        in_specs=in_specs,
        out_specs=out_specs,
        out_shape=[jax.ShapeDtypeStruct(x.shape, F32),
                   jax.ShapeDtypeStruct((batch, HIST, POOL_WIDTH), F32)],
        scratch_shapes=[pltpu.VMEM((HIST + TM, POOL_WIDTH), F32),
                        pltpu.VMEM((TM, D_MODEL), BF16)],
        compiler_params=pltpu.CompilerParams(
            dimension_semantics=("arbitrary", "arbitrary"), vmem_limit_bytes=VMEM_LIMIT),
        name=f"prompt_layer{layer}",
    )(x, mod4, g_mix, w_in, w_pool, ps, ln_g, ln_b, w_s, bs_full, w_out, g_ffn,
      w_gate, w_up, w_down, g_fin)


def _sample_kernel(x_ref, mod_ref, cache_ref, g_mix_ref, w_in_ref, w_pool_ref, ps_ref, ln_g_ref,
                   ln_b_ref, bd_ref, bs_ref, w_out_ref, g_ffn_ref, w_gate_ref, w_up_ref,
                   w_down_ref, g_fin_ref, o_ref, hist_ref, v_ref, pbuf_ref, ybuf_ref, *, final):
    ns, ls, _ = x_ref.shape
    m = ns * ls

    def rows(k):
        return jnp.broadcast_to(mod_ref[:, k:k + 1, :], (ns, ls, D_MODEL)).reshape(m, D_MODEL)

    x = x_ref[...].reshape(m, D_MODEL)
    p, u, v = _mixer_in(x, rows(0), rows(1), g_mix_ref[...], w_in_ref, ln_g_ref[...], ln_b_ref[...])
    v_ref[...] = v.reshape(ns, ls, GMLP_WIDTH)

    p3 = p.reshape(ns, ls, POOL_WIDTH)
    pbuf_ref[:, 0:HIST, :] = cache_ref[...]
    pbuf_ref[:, HIST:HIST + ls, :] = p3
    hist_ref[...] = p3
    for g, w in enumerate(POOL_WINDOWS):
        cols = pl.ds(g * POOL_GROUP, POOL_GROUP)
        acc = pbuf_ref[:, pl.ds(HIST, ls), cols]
        for j in range(1, w):
            acc = acc + pbuf_ref[:, pl.ds(HIST - j, ls), cols]
        d = (acc * (1.0 / w)).reshape(m, POOL_GROUP) - p[:, g * POOL_GROUP:(g + 1) * POOL_GROUP]
        y = _dot(d.astype(BF16), w_pool_ref[g]) * ps_ref[:, g * POOL_GROUP:(g + 1) * POOL_GROUP]
        ybuf_ref[:, g * POOL_GROUP:(g + 1) * POOL_GROUP] = y.astype(BF16)

    vb = v.astype(BF16)
    for hd in range(GMLP_HEADS):
        hc = slice(hd * GMLP_HEAD, (hd + 1) * GMLP_HEAD)
        zc = _dot(bd_ref[hd], vb[:, hc]) + bs_ref[:, hc]
        ybuf_ref[:, POOL_WIDTH + hd * GMLP_HEAD:POOL_WIDTH + (hd + 1) * GMLP_HEAD] = (
            u[:, hc] * zc).astype(BF16)

    x1 = x + rows(2) * _dot(ybuf_ref[...], w_out_ref[...])
    x2 = _ffn(x1, rows(3), rows(4), rows(5), g_ffn_ref[...], w_gate_ref, w_up_ref, w_down_ref)
    if final:
        x2 = _rms_norm(x2, g_fin_ref[...])
    o_ref[...] = x2.reshape(ns, ls, D_MODEL)


def _sample_call(layer, final, x, mod4, cache_pad, vecs, mats):
    g_mix, ps, ln_g, ln_b, g_ffn, g_fin, bs_s = vecs
    w_in, w_pool, bd, w_out, w_gate, w_up, w_down = mats
    ns, ls, _ = x.shape
    m = ns * ls
    l = layer
    vec = lambda n: pl.BlockSpec((None, 1, n), lambda i: (l, 0, 0))
    in_specs = [
        pl.BlockSpec((ns, ls, D_MODEL), lambda i: (0, 0, 0)),
        pl.BlockSpec((None, ns, N_MOD, D_MODEL), lambda i: (l, 0, 0, 0)),
        pl.BlockSpec((None, ns, HIST, POOL_WIDTH), lambda i: (l, 0, 0, 0)),
        vec(D_MODEL),
        pl.BlockSpec((None, D_MODEL, D_IN), lambda i: (l, 0, 0)),
        pl.BlockSpec((None, len(POOL_WINDOWS), POOL_GROUP, POOL_GROUP), lambda i: (l, 0, 0, 0)),
        vec(POOL_WIDTH), vec(GMLP_WIDTH), vec(GMLP_WIDTH),
        pl.BlockSpec((None, GMLP_HEADS, m, m), lambda i: (l, 0, 0, 0)),
        pl.BlockSpec((None, m, GMLP_WIDTH), lambda i: (l, 0, 0)),
        pl.BlockSpec((None, D_MODEL, D_MODEL), lambda i: (l, 0, 0)),
        vec(D_MODEL),
        pl.BlockSpec((None, D_MODEL, D_FF), lambda i: (l, 0, 0)),
        pl.BlockSpec((None, D_MODEL, D_FF), lambda i: (l, 0, 0)),
        pl.BlockSpec((None, D_FF, D_MODEL), lambda i: (l, 0, 0)),
        pl.BlockSpec((1, D_MODEL), lambda i: (0, 0)),
    ]
    out_specs = [
        pl.BlockSpec((ns, ls, D_MODEL), lambda i: (0, 0, 0)),
        pl.BlockSpec((ns, ls, POOL_WIDTH), lambda i: (0, 0, 0)),
        pl.BlockSpec((ns, ls, GMLP_WIDTH), lambda i: (0, 0, 0)),
    ]
    return pl.pallas_call(
        functools.partial(_sample_kernel, final=final),
        grid=(1,),
        in_specs=in_specs,
        out_specs=out_specs,
        out_shape=[jax.ShapeDtypeStruct(x.shape, F32),
                   jax.ShapeDtypeStruct((ns, ls, POOL_WIDTH), F32),
                   jax.ShapeDtypeStruct((ns, ls, GMLP_WIDTH), F32)],
        scratch_shapes=[pltpu.VMEM((ns, HIST + ls, POOL_WIDTH), F32),
                        pltpu.VMEM((m, D_MODEL), BF16)],
        compiler_params=pltpu.CompilerParams(
            dimension_semantics=("arbitrary",), vmem_limit_bytes=VMEM_LIMIT),
        name=f"sample_layer{layer}",
    )(x, mod4, cache_pad, g_mix, w_in, w_pool, ps, ln_g, ln_b, bd, bs_s, w_out, g_ffn,
      w_gate, w_up, w_down, g_fin)


def kernel(x_prompt, x_sample, c_prompt, c_sample, cache_pool, w_ada, b_ada, g_mix, w_in, w_pool,
           pool_scale, ln_g, ln_b, w_s, b_s, w_out, g_ffn, w_gate, w_up, w_down, g_final):
    depth = w_ada.shape[0]
    ns, ls, _ = x_sample.shape
    assert ls == HIST and x_prompt.shape[1] % TM == 0 and TM % GMLP_LEN == 0

    c_all = jnp.concatenate([c_sample, c_prompt], axis=0)
    pad = (-c_all.shape[0]) % 8
    c_all = jnp.pad(c_all, ((0, pad), (0, 0)))
    mod4 = _ada_call(c_all, w_ada, b_ada).reshape(depth, c_all.shape[0], N_MOD, D_MODEL)

    vec = lambda a: a.reshape(depth, 1, -1)
    w_in_b, w_pool_b, w_out_b = w_in.astype(BF16), w_pool.astype(BF16), w_out.astype(BF16)
    w_gate_b, w_up_b, w_down_b = w_gate.astype(BF16), w_up.astype(BF16), w_down.astype(BF16)
    g_fin = g_final.reshape(1, D_MODEL)
    bs_full = jnp.repeat(jnp.swapaxes(b_s, 1, 2), GMLP_HEAD, axis=2)
    bs_s = jnp.tile(bs_full[:, :ls, :], (1, ns, 1))
    w_s16 = jnp.tile(w_s[:, :, :ls, :ls], (1, 1, ns, ns))
    blk = jnp.arange(ns * ls) // ls
    bd = jnp.where(blk[:, None] == blk[None, :], w_s16, 0.0).astype(BF16)
    cache_pad = jnp.pad(cache_pool, ((0, 0), (0, 0), (HIST - cache_pool.shape[2], 0), (0, 0)))

    vecs_p = (vec(g_mix), vec(pool_scale), vec(ln_g), vec(ln_b), vec(g_ffn), g_fin, bs_full)
    vecs_s = vecs_p[:-1] + (bs_s,)
    mats_p = (w_in_b, w_pool_b, w_s, w_out_b, w_gate_b, w_up_b, w_down_b)
    mats_s = (w_in_b, w_pool_b, bd, w_out_b, w_gate_b, w_up_b, w_down_b)

    xp, xs = x_prompt, x_sample
    pool_p, pool_s, v_s = [], [], []
    for l in range(depth):
        final = l == depth - 1
        xp, hp = _prompt_call(l, final, xp, mod4, ns, vecs_p, mats_p)
        xs, hs, vs = _sample_call(l, final, xs, mod4, cache_pad, vecs_s, mats_s)
        pool_p.append(hp[:, 1:, :])
        pool_s.append(hs[:, 1:, :])
        v_s.append(vs)
    return (xp, xs, jnp.stack(pool_p), jnp.stack(pool_s), jnp.stack(v_s))
```

```python
import functools

import jax
import jax.numpy as jnp
from jax import lax
from jax.experimental import pallas as pl
from jax.experimental.pallas import tpu as pltpu

D_MODEL = 1024
POOL_WIDTH = 512
POOL_WINDOWS = (2, 4, 8, 16)
POOL_GROUP = 128
HIST = 16
GMLP_WIDTH = 512
GMLP_HEADS = 4
GMLP_HEAD = 128
GMLP_LEN = 128
CHUNK = 64
D_IN = 1536
D_FF = 2816
N_MOD = 6
EPS = 1e-6

TM = 512
ADA_TN = 2048
VMEM_LIMIT = 56 * 1024 * 1024

BF16 = jnp.bfloat16
F32 = jnp.float32


def _dot(a, b):
    return jnp.dot(a, b, preferred_element_type=F32)


def _rms_norm(x, g):
    return x * lax.rsqrt(jnp.mean(x * x, axis=-1, keepdims=True) + EPS) * g


def _layer_norm(x, g, b):
    mu = jnp.mean(x, axis=-1, keepdims=True)
    xc = x - mu
    var = jnp.mean(xc * xc, axis=-1, keepdims=True)
    return xc * lax.rsqrt(var + EPS) * g + b


def _ada_kernel(c_ref, w_ref, b_ref, o_ref):
    s = jax.nn.silu(c_ref[...]).astype(BF16)
    o_ref[...] = _dot(s, w_ref[...].astype(BF16)) + b_ref[...]


def _ada_call(c_all, w_ada, b_ada):
    depth = w_ada.shape[0]
    rows = c_all.shape[0]
    n_out = w_ada.shape[2]
    return pl.pallas_call(
        _ada_kernel,
        grid=(depth, n_out // ADA_TN),
        in_specs=[
            pl.BlockSpec((rows, D_MODEL), lambda l, j: (0, 0)),
            pl.BlockSpec((None, D_MODEL, ADA_TN), lambda l, j: (l, 0, j)),
            pl.BlockSpec((None, 1, ADA_TN), lambda l, j: (l, 0, j)),
        ],
        out_specs=pl.BlockSpec((None, rows, ADA_TN), lambda l, j: (l, 0, j)),
        out_shape=jax.ShapeDtypeStruct((depth, rows, n_out), F32),
        compiler_params=pltpu.CompilerParams(
            dimension_semantics=("arbitrary", "arbitrary"), vmem_limit_bytes=VMEM_LIMIT),
        name="ada_mod",
    )(c_all, w_ada, b_ada.reshape(depth, 1, n_out))


def _mixer_in(x, sh, sc, g_mix, w_in_ref, ln_g, ln_b):
    h = (_rms_norm(x, g_mix) * (1.0 + sc) + sh).astype(BF16)
    z = _dot(h, w_in_ref[...])
    p = z[:, :POOL_WIDTH]
    u = jax.nn.gelu(z[:, POOL_WIDTH:POOL_WIDTH + GMLP_WIDTH])
    v = _layer_norm(jax.nn.gelu(z[:, POOL_WIDTH + GMLP_WIDTH:]), ln_g, ln_b)
    return p, u, v


def _ffn(x1, sh, sc, gt, g_ffn, w_gate_ref, w_up_ref, w_down_ref):
    h = (_rms_norm(x1, g_ffn) * (1.0 + sc) + sh).astype(BF16)
    a = (jax.nn.silu(_dot(h, w_gate_ref[...])) * _dot(h, w_up_ref[...])).astype(BF16)
    return x1 + gt * _dot(a, w_down_ref[...])


def _prompt_kernel(x_ref, modm_ref, modf_ref, g_mix_ref, w_in_ref, w_pool_ref, ps_ref, ln_g_ref,
                   ln_b_ref, w_s_ref, bs_ref, w_out_ref, g_ffn_ref, w_gate_ref, w_up_ref,
                   w_down_ref, g_fin_ref, o_ref, hist_ref, pbuf_ref, ybuf_ref, x1_ref, h2_ref,
                   *, final, tiles_per_seq):
    t = pl.program_id(0)
    i = jnp.minimum(t, pl.num_programs(0) - 2) % tiles_per_seq
    tm = x_ref.shape[0]

    @pl.when(t == 0)
    def _():
        x1_ref[...] = jnp.zeros(x1_ref.shape, F32)
        h2_ref[...] = jnp.zeros(h2_ref.shape, BF16)

    @pl.when(i == 0)
    def _():
        pbuf_ref[0:HIST, :] = jnp.zeros((HIST, POOL_WIDTH), F32)

    x = x_ref[...]
    sh_m, sc_m, gt_m = modm_ref[0:1, :], modm_ref[1:2, :], modm_ref[2:3, :]
    sh_f, sc_f = modm_ref[3:4, :], modm_ref[4:5, :]
    gt_f_prev = modf_ref[5:6, :]

    h = (_rms_norm(x, g_mix_ref[...]) * (1.0 + sc_m) + sh_m).astype(BF16)
    h2_prev = h2_ref[...]
    gate = _dot(h2_prev, w_gate_ref[...])
    z = _dot(h, w_in_ref[...])
    up = _dot(h2_prev, w_up_ref[...])

    p = z[:, :POOL_WIDTH]
    u = jax.nn.gelu(z[:, POOL_WIDTH:POOL_WIDTH + GMLP_WIDTH])
    v = _layer_norm(jax.nn.gelu(z[:, POOL_WIDTH + GMLP_WIDTH:]), ln_g_ref[...], ln_b_ref[...])

    pbuf_ref[HIST:HIST + tm, :] = p
    row = lax.broadcasted_iota(jnp.int32, (HIST, POOL_GROUP), 0) + i * tm
    for g, w in enumerate(POOL_WINDOWS):
        cols = pl.ds(g * POOL_GROUP, POOL_GROUP)
        acc = pbuf_ref[pl.ds(HIST, tm), cols]
        for j in range(1, w):
            acc = acc + pbuf_ref[pl.ds(HIST - j, tm), cols]
        pg = p[:, g * POOL_GROUP:(g + 1) * POOL_GROUP]
        cnt = jnp.minimum(row + 1, w).astype(F32)
        d = jnp.concatenate([acc[:HIST] / cnt, acc[HIST:] * (1.0 / w)], axis=0) - pg
        y = _dot(d.astype(BF16), w_pool_ref[g]) * ps_ref[:, g * POOL_GROUP:(g + 1) * POOL_GROUP]
        ybuf_ref[:, g * POOL_GROUP:(g + 1) * POOL_GROUP] = y.astype(BF16)
    last = p[tm - HIST:, :]
    pbuf_ref[0:HIST, :] = last
    hist_ref[...] = last

    r_idx = lax.broadcasted_iota(jnp.int32, (GMLP_LEN, GMLP_LEN), 0) // CHUNK
    c_idx = lax.broadcasted_iota(jnp.int32, (GMLP_LEN, GMLP_LEN), 1) // CHUNK
    vb = v.astype(BF16)
    for hd in range(GMLP_HEADS):
        wm = jnp.where(c_idx <= r_idx, w_s_ref[hd], 0.0).astype(BF16)
        hc = slice(hd * GMLP_HEAD, (hd + 1) * GMLP_HEAD)
        bias = bs_ref[:, hc]
        for c in range(tm // GMLP_LEN):
            rc = slice(c * GMLP_LEN, (c + 1) * GMLP_LEN)
            zc = _dot(wm, vb[rc, hc]) + bias
            ybuf_ref[rc, POOL_WIDTH + hd * GMLP_HEAD:POOL_WIDTH + (hd + 1) * GMLP_HEAD] = (
                u[rc, hc] * zc).astype(BF16)

    a = (jax.nn.silu(gate) * up).astype(BF16)
    mix = _dot(ybuf_ref[...], w_out_ref[...])
    f = _dot(a, w_down_ref[...])

    x1 = x + gt_m * mix
    out = x1_ref[...] + gt_f_prev * f
    if final:
        out = _rms_norm(out, g_fin_ref[...])
    o_ref[...] = out
    x1_ref[...] = x1
    h2_ref[...] = (_rms_norm(x1, g_ffn_ref[...]) * (1.0 + sc_f) + sh_f).astype(BF16)


def _resident(shape, index_map):
    return pl.BlockSpec(shape, index_map, pipeline_mode=pl.Buffered(1))


def _prompt_call(layer, final, x, mod4, n_sample, vecs, mats):
    g_mix, ps, ln_g, ln_b, g_ffn, g_fin, bs_full = vecs
    w_in, w_pool, w_s, w_out, w_gate, w_up, w_down = mats
    batch, seq, _ = x.shape
    nt = seq // TM
    n_tiles = batch * nt
    l = layer
    mix_tile = lambda t: jnp.minimum(t, n_tiles - 1)
    ffn_tile = lambda t: jnp.maximum(t - 1, 0)
    vec = lambda n: _resident((None, 1, n), lambda t: (l, 0, 0))
    in_specs = [
        pl.BlockSpec((None, TM, D_MODEL), lambda t: (mix_tile(t) // nt, mix_tile(t) % nt, 0)),
        pl.BlockSpec((None, None, N_MOD, D_MODEL), lambda t: (l, n_sample + mix_tile(t) // nt, 0, 0)),
        pl.BlockSpec((None, None, N_MOD, D_MODEL), lambda t: (l, n_sample + ffn_tile(t) // nt, 0, 0)),
        vec(D_MODEL),
        _resident((None, D_MODEL, D_IN), lambda t: (l, 0, 0)),
        _resident((None, len(POOL_WINDOWS), POOL_GROUP, POOL_GROUP), lambda t: (l, 0, 0, 0)),
        vec(POOL_WIDTH), vec(GMLP_WIDTH), vec(GMLP_WIDTH),
        _resident((None, GMLP_HEADS, GMLP_LEN, GMLP_LEN), lambda t: (l, 0, 0, 0)),
        _resident((None, GMLP_LEN, GMLP_WIDTH), lambda t: (l, 0, 0)),
        _resident((None, D_MODEL, D_MODEL), lambda t: (l, 0, 0)),
        vec(D_MODEL),
        _resident((None, D_MODEL, D_FF), lambda t: (l, 0, 0)),
        _resident((None, D_MODEL, D_FF), lambda t: (l, 0, 0)),
        _resident((None, D_FF, D_MODEL), lambda t: (l, 0, 0)),
        _resident((1, D_MODEL), lambda t: (0, 0)),
    ]
    out_specs = [
        pl.BlockSpec((None, TM, D_MODEL), lambda t: (ffn_tile(t) // nt, ffn_tile(t) % nt, 0)),
        pl.BlockSpec((None, HIST, POOL_WIDTH), lambda t: (mix_tile(t) // nt, 0, 0)),
    ]
    return pl.pallas_call(
        functools.partial(_prompt_kernel, final=final, tiles_per_seq=nt),
        grid=(n_tiles + 1,),
        in_specs=in_specs,
        out_specs=out_specs,
        out_shape=[jax.ShapeDtypeStruct(x.shape, F32),
                   jax.ShapeDtypeStruct((batch, HIST, POOL_WIDTH), F32)],
        scratch_shapes=[pltpu.VMEM((HIST + TM, POOL_WIDTH), F32),
                        pltpu.VMEM((TM, D_MODEL), BF16),
                        pltpu.VMEM((TM, D_MODEL), F32),
                        pltpu.VMEM((TM, D_MODEL), BF16)],
        compiler_params=pltpu.CompilerParams(
            dimension_semantics=("arbitrary",), vmem_limit_bytes=VMEM_LIMIT),
        name=f"prompt_layer{layer}",
    )(x, mod4, mod4, g_mix, w_in, w_pool, ps, ln_g, ln_b, w_s, bs_full, w_out, g_ffn,
      w_gate, w_up, w_down, g_fin)


def _sample_kernel(x_ref, mod_ref, cache_ref, g_mix_ref, w_in_ref, w_pool_ref, ps_ref, ln_g_ref,
                   ln_b_ref, bd_ref, bs_ref, w_out_ref, g_ffn_ref, w_gate_ref, w_up_ref,
                   w_down_ref, g_fin_ref, o_ref, hist_ref, v_ref, pbuf_ref, ybuf_ref, *, final):
    ns, ls, _ = x_ref.shape
    m = ns * ls

    def rows(k):
        return jnp.broadcast_to(mod_ref[:, k:k + 1, :], (ns, ls, D_MODEL)).reshape(m, D_MODEL)

    x = x_ref[...].reshape(m, D_MODEL)
    p, u, v = _mixer_in(x, rows(0), rows(1), g_mix_ref[...], w_in_ref, ln_g_ref[...], ln_b_ref[...])
    v_ref[...] = v.reshape(ns, ls, GMLP_WIDTH)

    p3 = p.reshape(ns, ls, POOL_WIDTH)
    pbuf_ref[:, 0:HIST, :] = cache_ref[...]
    pbuf_ref[:, HIST:HIST + ls, :] = p3
    hist_ref[...] = p3
    for g, w in enumerate(POOL_WINDOWS):
        cols = pl.ds(g * POOL_GROUP, POOL_GROUP)
        acc = pbuf_ref[:, pl.ds(HIST, ls), cols]
        for j in range(1, w):
            acc = acc + pbuf_ref[:, pl.ds(HIST - j, ls), cols]
        d = (acc * (1.0 / w)).reshape(m, POOL_GROUP) - p[:, g * POOL_GROUP:(g + 1) * POOL_GROUP]
        y = _dot(d.astype(BF16), w_pool_ref[g]) * ps_ref[:, g * POOL_GROUP:(g + 1) * POOL_GROUP]
        ybuf_ref[:, g * POOL_GROUP:(g + 1) * POOL_GROUP] = y.astype(BF16)

    vb = v.astype(BF16)
    for hd in range(GMLP_HEADS):
        hc = slice(hd * GMLP_HEAD, (hd + 1) * GMLP_HEAD)
        zc = _dot(bd_ref[hd], vb[:, hc]) + bs_ref[:, hc]
        ybuf_ref[:, POOL_WIDTH + hd * GMLP_HEAD:POOL_WIDTH + (hd + 1) * GMLP_HEAD] = (
            u[:, hc] * zc).astype(BF16)

    x1 = x + rows(2) * _dot(ybuf_ref[...], w_out_ref[...])
    x2 = _ffn(x1, rows(3), rows(4), rows(5), g_ffn_ref[...], w_gate_ref, w_up_ref, w_down_ref)
    if final:
        x2 = _rms_norm(x2, g_fin_ref[...])
    o_ref[...] = x2.reshape(ns, ls, D_MODEL)


def _sample_call(layer, final, x, mod4, cache_pad, vecs, mats):
    g_mix, ps, ln_g, ln_b, g_ffn, g_fin, bs_s = vecs
    w_in, w_pool, bd, w_out, w_gate, w_up, w_down = mats
    ns, ls, _ = x.shape
    m = ns * ls
    l = layer
    vec = lambda n: pl.BlockSpec((None, 1, n), lambda i: (l, 0, 0))
    in_specs = [
        pl.BlockSpec((ns, ls, D_MODEL), lambda i: (0, 0, 0)),
        pl.BlockSpec((None, ns, N_MOD, D_MODEL), lambda i: (l, 0, 0, 0)),
        pl.BlockSpec((None, ns, HIST, POOL_WIDTH), lambda i: (l, 0, 0, 0)),
        vec(D_MODEL),
        pl.BlockSpec((None, D_MODEL, D_IN), lambda i: (l, 0, 0)),
        pl.BlockSpec((None, len(POOL_WINDOWS), POOL_GROUP, POOL_GROUP), lambda i: (l, 0, 0, 0)),
        vec(POOL_WIDTH), vec(GMLP_WIDTH), vec(GMLP_WIDTH),
        pl.BlockSpec((None, GMLP_HEADS, m, m), lambda i: (l, 0, 0, 0)),
        pl.BlockSpec((None, m, GMLP_WIDTH), lambda i: (l, 0, 0)),
        pl.BlockSpec((None, D_MODEL, D_MODEL), lambda i: (l, 0, 0)),
        vec(D_MODEL),
        pl.BlockSpec((None, D_MODEL, D_FF), lambda i: (l, 0, 0)),
        pl.BlockSpec((None, D_MODEL, D_FF), lambda i: (l, 0, 0)),
        pl.BlockSpec((None, D_FF, D_MODEL), lambda i: (l, 0, 0)),
        pl.BlockSpec((1, D_MODEL), lambda i: (0, 0)),
    ]
    out_specs = [
        pl.BlockSpec((ns, ls, D_MODEL), lambda i: (0, 0, 0)),
        pl.BlockSpec((ns, ls, POOL_WIDTH), lambda i: (0, 0, 0)),
        pl.BlockSpec((ns, ls, GMLP_WIDTH), lambda i: (0, 0, 0)),
    ]
    return pl.pallas_call(
        functools.partial(_sample_kernel, final=final),
        grid=(1,),
        in_specs=in_specs,
        out_specs=out_specs,
        out_shape=[jax.ShapeDtypeStruct(x.shape, F32),
                   jax.ShapeDtypeStruct((ns, ls, POOL_WIDTH), F32),
                   jax.ShapeDtypeStruct((ns, ls, GMLP_WIDTH), F32)],
        scratch_shapes=[pltpu.VMEM((ns, HIST + ls, POOL_WIDTH), F32),
                        pltpu.VMEM((m, D_MODEL), BF16)],
        compiler_params=pltpu.CompilerParams(
            dimension_semantics=("arbitrary",), vmem_limit_bytes=VMEM_LIMIT),
        name=f"sample_layer{layer}",
    )(x, mod4, cache_pad, g_mix, w_in, w_pool, ps, ln_g, ln_b, bd, bs_s, w_out, g_ffn,
      w_gate, w_up, w_down, g_fin)


def kernel(x_prompt, x_sample, c_prompt, c_sample, cache_pool, w_ada, b_ada, g_mix, w_in, w_pool,
           pool_scale, ln_g, ln_b, w_s, b_s, w_out, g_ffn, w_gate, w_up, w_down, g_final):
    depth = w_ada.shape[0]
    ns, ls, _ = x_sample.shape
    assert ls == HIST and x_prompt.shape[1] % TM == 0 and TM % GMLP_LEN == 0

    c_all = jnp.concatenate([c_sample, c_prompt], axis=0)
    pad = (-c_all.shape[0]) % 8
    c_all = jnp.pad(c_all, ((0, pad), (0, 0)))
    mod4 = _ada_call(c_all, w_ada, b_ada).reshape(depth, c_all.shape[0], N_MOD, D_MODEL)

    vec = lambda a: a.reshape(depth, 1, -1)
    w_in_b, w_pool_b, w_out_b = w_in.astype(BF16), w_pool.astype(BF16), w_out.astype(BF16)
    w_gate_b, w_up_b, w_down_b = w_gate.astype(BF16), w_up.astype(BF16), w_down.astype(BF16)
    g_fin = g_final.reshape(1, D_MODEL)
    bs_full = jnp.repeat(jnp.swapaxes(b_s, 1, 2), GMLP_HEAD, axis=2)
    bs_s = jnp.tile(bs_full[:, :ls, :], (1, ns, 1))
    w_s16 = jnp.tile(w_s[:, :, :ls, :ls], (1, 1, ns, ns))
    blk = jnp.arange(ns * ls) // ls
    bd = jnp.where(blk[:, None] == blk[None, :], w_s16, 0.0).astype(BF16)
    cache_pad = jnp.pad(cache_pool, ((0, 0), (0, 0), (HIST - cache_pool.shape[2], 0), (0, 0)))

    vecs_p = (vec(g_mix), vec(pool_scale), vec(ln_g), vec(ln_b), vec(g_ffn), g_fin, bs_full)
    vecs_s = vecs_p[:-1] + (bs_s,)
    mats_p = (w_in_b, w_pool_b, w_s, w_out_b, w_gate_b, w_up_b, w_down_b)
    mats_s = (w_in_b, w_pool_b, bd, w_out_b, w_gate_b, w_up_b, w_down_b)

    xp, xs = x_prompt, x_sample
    pool_p, pool_s, v_s = [], [], []
    for l in range(depth):
        final = l == depth - 1
        xp, hp = _prompt_call(l, final, xp, mod4, ns, vecs_p, mats_p)
        xs, hs, vs = _sample_call(l, final, xs, mod4, cache_pad, vecs_s, mats_s)
        pool_p.append(hp[:, 1:, :])
        pool_s.append(hs[:, 1:, :])
        v_s.append(vs)
    return (xp, xs, jnp.stack(pool_p), jnp.stack(pool_s), jnp.stack(v_s))
```

```python
import functools

import jax
import jax.numpy as jnp
from jax import lax
from jax.experimental import pallas as pl
from jax.experimental.pallas import tpu as pltpu

D_MODEL = 1024
POOL_WIDTH = 512
POOL_WINDOWS = (2, 4, 8, 16)
POOL_GROUP = 128
HIST = 16
GMLP_WIDTH = 512
GMLP_HEADS = 4
GMLP_HEAD = 128
GMLP_LEN = 128
CHUNK = 64
D_IN = 1536
D_FF = 2816
N_MOD = 6
EPS = 1e-6

TM = 512
ADA_TN = 2048
W_ROWS = 128
W_SLOTS = 2
VMEM_LIMIT = 60 * 1024 * 1024

BF16 = jnp.bfloat16
F32 = jnp.float32


def _dot(a, b):
    return jnp.dot(a, b, preferred_element_type=F32)


def _rms_norm(x, g):
    return x * lax.rsqrt(jnp.mean(x * x, axis=-1, keepdims=True) + EPS) * g


def _layer_norm(x, g, b):
    mu = jnp.mean(x, axis=-1, keepdims=True)
    xc = x - mu
    var = jnp.mean(xc * xc, axis=-1, keepdims=True)
    return xc * lax.rsqrt(var + EPS) * g + b


def _ada_kernel(c_ref, w_ref, b_ref, o_ref):
    s = jax.nn.silu(c_ref[...]).astype(BF16)
    o_ref[...] = _dot(s, w_ref[...].astype(BF16)) + b_ref[...]


def _ada_call(c_all, w_ada, b_ada):
    depth = w_ada.shape[0]
    rows = c_all.shape[0]
    n_out = w_ada.shape[2]
    return pl.pallas_call(
        _ada_kernel,
        grid=(depth, n_out // ADA_TN),
        in_specs=[
            pl.BlockSpec((rows, D_MODEL), lambda l, j: (0, 0)),
            pl.BlockSpec((None, D_MODEL, ADA_TN), lambda l, j: (l, 0, j)),
            pl.BlockSpec((None, 1, ADA_TN), lambda l, j: (l, 0, j)),
        ],
        out_specs=pl.BlockSpec((None, rows, ADA_TN), lambda l, j: (l, 0, j)),
        out_shape=jax.ShapeDtypeStruct((depth, rows, n_out), F32),
        compiler_params=pltpu.CompilerParams(
            dimension_semantics=("arbitrary", "arbitrary"), vmem_limit_bytes=VMEM_LIMIT),
        name="ada_mod",
    )(c_all, w_ada, b_ada.reshape(depth, 1, n_out))


def _layer_kernel(x_ref, modm_ref, modf_ref, xs_ref, mods_ref, cache_ref, g_mix_ref, w_pool_ref,
                  ps_ref, ln_g_ref, ln_b_ref, w_s_ref, bs_ref, bd_ref, bss_ref, g_ffn_ref, g_fin_ref,
                  w_in_hbm, w_out_hbm, w_gate_hbm, w_up_hbm, w_down_hbm,
                  o_ref, hist_ref, os_ref, hists_ref, vs_ref,
                  w_in_v, w_out_v, w_gate_v, w_up_v, w_down_v, stage_ref, dma_sem,
                  pbuf_ref, ybuf_ref, x1_ref, h2_ref, pbufs_ref, x1s_ref, h2s_ref,
                  *, layer, final, tiles_per_seq, n_tiles):
    t = pl.program_id(0)
    i = jnp.minimum(t, n_tiles - 1) % tiles_per_seq
    tm = x_ref.shape[0]
    ns, ls, _ = xs_ref.shape
    ms = ns * ls

    def load_weights():
        copies = []
        for hbm, vmem in ((w_in_hbm, w_in_v), (w_out_hbm, w_out_v), (w_gate_hbm, w_gate_v),
                          (w_up_hbm, w_up_v), (w_down_hbm, w_down_v)):
            k, n = vmem.shape
            for r0 in range(0, k, W_ROWS):
                copies.append((hbm, vmem, r0, n))

        def copy(c):
            hbm, _, r0, n = copies[c]
            slot = c % W_SLOTS
            return pltpu.make_async_copy(hbm.at[layer, pl.ds(r0, W_ROWS), :],
                                         stage_ref.at[slot, :, pl.ds(0, n)], dma_sem.at[slot])

        for c in range(min(W_SLOTS, len(copies))):
            copy(c).start()
        for c, (_, vmem, r0, n) in enumerate(copies):
            copy(c).wait()
            vmem[pl.ds(r0, W_ROWS), :] = stage_ref[c % W_SLOTS, :, pl.ds(0, n)].astype(BF16)
            if c + W_SLOTS < len(copies):
                copy(c + W_SLOTS).start()

    def sample_rows(k):
        return jnp.broadcast_to(mods_ref[:, k:k + 1, :], (ns, ls, D_MODEL)).reshape(ms, D_MODEL)

    def pool_prompt(p):
        pbuf_ref[HIST:HIST + tm, :] = p
        row = lax.broadcasted_iota(jnp.int32, (HIST, POOL_GROUP), 0) + i * tm
        for g, w in enumerate(POOL_WINDOWS):
            cols = pl.ds(g * POOL_GROUP, POOL_GROUP)
            acc = pbuf_ref[pl.ds(HIST, tm), cols]
            for j in range(1, w):
                acc = acc + pbuf_ref[pl.ds(HIST - j, tm), cols]
            pg = p[:, g * POOL_GROUP:(g + 1) * POOL_GROUP]
            cnt = jnp.minimum(row + 1, w).astype(F32)
            d = jnp.concatenate([acc[:HIST] / cnt, acc[HIST:] * (1.0 / w)], axis=0) - pg
            y = (_dot(d.astype(BF16), w_pool_ref[g].astype(BF16))
                 * ps_ref[:, g * POOL_GROUP:(g + 1) * POOL_GROUP])
            ybuf_ref[:, g * POOL_GROUP:(g + 1) * POOL_GROUP] = y.astype(BF16)
        last = p[tm - HIST:, :]
        pbuf_ref[0:HIST, :] = last
        hist_ref[...] = last

    def pool_sample(p):
        p3 = p.reshape(ns, ls, POOL_WIDTH)
        pbufs_ref[:, 0:HIST, :] = cache_ref[...]
        pbufs_ref[:, HIST:HIST + ls, :] = p3
        hists_ref[...] = p3
        for g, w in enumerate(POOL_WINDOWS):
            cols = pl.ds(g * POOL_GROUP, POOL_GROUP)
            acc = pbufs_ref[:, pl.ds(HIST, ls), cols]
            for j in range(1, w):
                acc = acc + pbufs_ref[:, pl.ds(HIST - j, ls), cols]
            d = (acc * (1.0 / w)).reshape(ms, POOL_GROUP) - p[:, g * POOL_GROUP:(g + 1) * POOL_GROUP]
            y = (_dot(d.astype(BF16), w_pool_ref[g].astype(BF16))
                 * ps_ref[:, g * POOL_GROUP:(g + 1) * POOL_GROUP])
            ybuf_ref[0:ms, g * POOL_GROUP:(g + 1) * POOL_GROUP] = y.astype(BF16)

    def gmlp_prompt(u, v):
        r_idx = lax.broadcasted_iota(jnp.int32, (GMLP_LEN, GMLP_LEN), 0) // CHUNK
        c_idx = lax.broadcasted_iota(jnp.int32, (GMLP_LEN, GMLP_LEN), 1) // CHUNK
        vb = v.astype(BF16)
        for hd in range(GMLP_HEADS):
            wm = jnp.where(c_idx <= r_idx, w_s_ref[hd], 0.0).astype(BF16)
            hc = slice(hd * GMLP_HEAD, (hd + 1) * GMLP_HEAD)
            bias = bs_ref[:, hc]
            for c in range(tm // GMLP_LEN):
                rc = slice(c * GMLP_LEN, (c + 1) * GMLP_LEN)
                zc = _dot(wm, vb[rc, hc]) + bias
                ybuf_ref[rc, POOL_WIDTH + hd * GMLP_HEAD:POOL_WIDTH + (hd + 1) * GMLP_HEAD] = (
                    u[rc, hc] * zc).astype(BF16)

    def gmlp_sample(u, v):
        vs_ref[...] = v.reshape(ns, ls, GMLP_WIDTH)
        vb = v.astype(BF16)
        for hd in range(GMLP_HEADS):
            hc = slice(hd * GMLP_HEAD, (hd + 1) * GMLP_HEAD)
            zc = _dot(bd_ref[hd], vb[:, hc]) + bss_ref[:, hc]
            ybuf_ref[0:ms, POOL_WIDTH + hd * GMLP_HEAD:POOL_WIDTH + (hd + 1) * GMLP_HEAD] = (
                u[:, hc] * zc).astype(BF16)

    def stage(mixer, ffn):
        if mixer == "prompt":
            x, mod = x_ref[...], (lambda k: modm_ref[k:k + 1, :])
        elif mixer == "sample":
            x, mod = xs_ref[...].reshape(ms, D_MODEL), sample_rows
        if mixer:
            h = (_rms_norm(x, g_mix_ref[...]) * (1.0 + mod(1)) + mod(0)).astype(BF16)
        if ffn:
            h2_prev = h2_ref[...] if ffn == "prompt" else h2s_ref[...]
            gate = _dot(h2_prev, w_gate_v[...])
        if mixer:
            z = _dot(h, w_in_v[...])
        if ffn:
            up = _dot(h2_prev, w_up_v[...])
        if mixer:
            p = z[:, :POOL_WIDTH]
            u = jax.nn.gelu(z[:, POOL_WIDTH:POOL_WIDTH + GMLP_WIDTH])
            v = _layer_norm(jax.nn.gelu(z[:, POOL_WIDTH + GMLP_WIDTH:]), ln_g_ref[...], ln_b_ref[...])
            if mixer == "prompt":
                pool_prompt(p)
                gmlp_prompt(u, v)
            else:
                pool_sample(p)
                gmlp_sample(u, v)
        if ffn:
            a = (jax.nn.silu(gate) * up).astype(BF16)
        if mixer:
            mix = _dot(ybuf_ref[0:x.shape[0], :], w_out_v[...])
        if ffn:
            f = _dot(a, w_down_v[...])
        if mixer:
            x1 = x + mod(2) * mix
        if ffn:
            if ffn == "prompt":
                out = x1_ref[...] + modf_ref[5:6, :] * f
            else:
                out = x1s_ref[...] + sample_rows(5) * f
            if final:
                out = _rms_norm(out, g_fin_ref[...])
            if ffn == "prompt":
                o_ref[...] = out
            else:
                os_ref[...] = out.reshape(ns, ls, D_MODEL)
        if mixer:
            h2 = (_rms_norm(x1, g_ffn_ref[...]) * (1.0 + mod(4)) + mod(3)).astype(BF16)
            if mixer == "prompt":
                x1_ref[...] = x1
                h2_ref[...] = h2
            else:
                x1s_ref[...] = x1
                h2s_ref[...] = h2

    @pl.when(i == 0)
    def _():
        pbuf_ref[0:HIST, :] = jnp.zeros((HIST, POOL_WIDTH), F32)

    @pl.when(t == 0)
    def _():
        load_weights()
        stage("prompt", None)

    @pl.when(jnp.logical_and(t > 0, t < n_tiles))
    def _():
        stage("prompt", "prompt")

    @pl.when(t == n_tiles)
    def _():
        stage("sample", "prompt")

    @pl.when(t == n_tiles + 1)
    def _():
        stage(None, "sample")


def _resident(shape, index_map):
    return pl.BlockSpec(shape, index_map, pipeline_mode=pl.Buffered(1))


def _layer_call(layer, final, xp, xs, mod4, cache_pad, small, weights):
    g_mix, w_pool, ps, ln_g, ln_b, w_s, bs_full, bd, bs_s, g_ffn, g_fin = small
    batch, seq, _ = xp.shape
    ns, ls, _ = xs.shape
    ms = ns * ls
    nt = seq // TM
    n_tiles = batch * nt
    l = layer
    mix_tile = lambda t: jnp.minimum(t, n_tiles - 1)
    ffn_tile = lambda t: jnp.clip(t - 1, 0, n_tiles - 1)
    vec = lambda n: _resident((None, 1, n), lambda t: (l, 0, 0))
    in_specs = [
        pl.BlockSpec((None, TM, D_MODEL), lambda t: (mix_tile(t) // nt, mix_tile(t) % nt, 0)),
        pl.BlockSpec((None, None, N_MOD, D_MODEL), lambda t: (l, ns + mix_tile(t) // nt, 0, 0)),
        pl.BlockSpec((None, None, N_MOD, D_MODEL), lambda t: (l, ns + ffn_tile(t) // nt, 0, 0)),
        _resident((ns, ls, D_MODEL), lambda t: (0, 0, 0)),
        _resident((None, ns, N_MOD, D_MODEL), lambda t: (l, 0, 0, 0)),
        _resident((None, ns, HIST, POOL_WIDTH), lambda t: (l, 0, 0, 0)),
        vec(D_MODEL),
        _resident((None, len(POOL_WINDOWS), POOL_GROUP, POOL_GROUP), lambda t: (l, 0, 0, 0)),
        vec(POOL_WIDTH), vec(GMLP_WIDTH), vec(GMLP_WIDTH),
        _resident((None, GMLP_HEADS, GMLP_LEN, GMLP_LEN), lambda t: (l, 0, 0, 0)),
        _resident((None, GMLP_LEN, GMLP_WIDTH), lambda t: (l, 0, 0)),
        _resident((None, GMLP_HEADS, ms, ms), lambda t: (l, 0, 0, 0)),
        _resident((None, ms, GMLP_WIDTH), lambda t: (l, 0, 0)),
        vec(D_MODEL),
        _resident((1, D_MODEL), lambda t: (0, 0)),
    ] + [pl.BlockSpec(memory_space=pl.ANY)] * 5
    out_specs = [
        pl.BlockSpec((None, TM, D_MODEL), lambda t: (ffn_tile(t) // nt, ffn_tile(t) % nt, 0)),
        pl.BlockSpec((None, HIST, POOL_WIDTH), lambda t: (mix_tile(t) // nt, 0, 0)),
        pl.BlockSpec((ns, ls, D_MODEL), lambda t: (0, 0, 0)),
        pl.BlockSpec((ns, ls, POOL_WIDTH), lambda t: (0, 0, 0)),
        pl.BlockSpec((ns, ls, GMLP_WIDTH), lambda t: (0, 0, 0)),
    ]
    out_shape = [
        jax.ShapeDtypeStruct(xp.shape, F32),
        jax.ShapeDtypeStruct((batch, HIST, POOL_WIDTH), F32),
        jax.ShapeDtypeStruct(xs.shape, F32),
        jax.ShapeDtypeStruct((ns, ls, POOL_WIDTH), F32),
        jax.ShapeDtypeStruct((ns, ls, GMLP_WIDTH), F32),
    ]
    scratch_shapes = [
        pltpu.VMEM((D_MODEL, D_IN), BF16),
        pltpu.VMEM((D_MODEL, D_MODEL), BF16),
        pltpu.VMEM((D_MODEL, D_FF), BF16),
        pltpu.VMEM((D_MODEL, D_FF), BF16),
        pltpu.VMEM((D_FF, D_MODEL), BF16),
        pltpu.VMEM((W_SLOTS, W_ROWS, D_FF), F32),
        pltpu.SemaphoreType.DMA((W_SLOTS,)),
        pltpu.VMEM((HIST + TM, POOL_WIDTH), F32),
        pltpu.VMEM((TM, D_MODEL), BF16),
        pltpu.VMEM((TM, D_MODEL), F32),
        pltpu.VMEM((TM, D_MODEL), BF16),
        pltpu.VMEM((ns, HIST + ls, POOL_WIDTH), F32),
        pltpu.VMEM((ms, D_MODEL), F32),
        pltpu.VMEM((ms, D_MODEL), BF16),
    ]
    return pl.pallas_call(
        functools.partial(_layer_kernel, layer=layer, final=final, tiles_per_seq=nt, n_tiles=n_tiles),
        grid=(n_tiles + 2,),
        in_specs=in_specs,
        out_specs=out_specs,
        out_shape=out_shape,
        scratch_shapes=scratch_shapes,
        compiler_params=pltpu.CompilerParams(
            dimension_semantics=("arbitrary",), vmem_limit_bytes=VMEM_LIMIT),
        name=f"layer{layer}",
    )(xp, mod4, mod4, xs, mod4, cache_pad, g_mix, w_pool, ps, ln_g, ln_b, w_s, bs_full, bd, bs_s,
      g_ffn, g_fin, *weights)


def kernel(x_prompt, x_sample, c_prompt, c_sample, cache_pool, w_ada, b_ada, g_mix, w_in, w_pool,
           pool_scale, ln_g, ln_b, w_s, b_s, w_out, g_ffn, w_gate, w_up, w_down, g_final):
    depth = w_ada.shape[0]
    ns, ls, _ = x_sample.shape
    ms = ns * ls
    assert ls == HIST and ms <= TM and x_prompt.shape[1] % TM == 0 and TM % GMLP_LEN == 0

    c_all = jnp.concatenate([c_sample, c_prompt], axis=0)
    pad = (-c_all.shape[0]) % 8
    c_all = jnp.pad(c_all, ((0, pad), (0, 0)))
    mod4 = _ada_call(c_all, w_ada, b_ada).reshape(depth, c_all.shape[0], N_MOD, D_MODEL)

    vec = lambda a: a.reshape(depth, 1, -1)
    g_fin = g_final.reshape(1, D_MODEL)
    bs_t = jnp.swapaxes(b_s, 1, 2)
    bs_full = jnp.broadcast_to(bs_t[..., None], bs_t.shape + (GMLP_HEAD,)).reshape(
        depth, GMLP_LEN, GMLP_WIDTH)
    bs_s = jnp.broadcast_to(bs_full[:, None, :ls, :], (depth, ns, ls, GMLP_WIDTH)).reshape(
        depth, ms, GMLP_WIDTH)
    same = jnp.eye(ns, dtype=bool)[None, None, :, None, :, None]
    bd = jnp.where(same, w_s[:, :, None, :ls, None, :ls], 0.0).astype(BF16).reshape(
        depth, GMLP_HEADS, ms, ms)
    cache_pad = jnp.pad(cache_pool, ((0, 0), (0, 0), (HIST - cache_pool.shape[2], 0), (0, 0)))

    small = (vec(g_mix), w_pool, vec(pool_scale), vec(ln_g), vec(ln_b), w_s, bs_full, bd, bs_s,
             vec(g_ffn), g_fin)
    weights = (w_in, w_out, w_gate, w_up, w_down)

    xp, xs = x_prompt, x_sample
    pool_p, pool_s, v_s = [], [], []
    for l in range(depth):
        xp, hp, xs, hs, vs = _layer_call(l, l == depth - 1, xp, xs, mod4, cache_pad, small, weights)
        pool_p.append(hp[:, 1:, :])
        pool_s.append(hs[:, 1:, :])
        v_s.append(vs)
    return (xp, xs, jnp.stack(pool_p), jnp.stack(pool_s), jnp.stack(v_s))
```

```python
import functools

import jax
import jax.numpy as jnp
from jax import lax
from jax.experimental import pallas as pl
from jax.experimental.pallas import tpu as pltpu

D_MODEL = 1024
POOL_WIDTH = 512
POOL_WINDOWS = (2, 4, 8, 16)
POOL_GROUP = 128
HIST = 16
GMLP_WIDTH = 512
GMLP_HEADS = 4
GMLP_HEAD = 128
GMLP_LEN = 128
CHUNK = 64
D_IN = 1536
D_FF = 2816
N_MOD = 6
EPS = 1e-6

TM = 512
ADA_TN = 2048
W_ROWS = 128
W_SLOTS = 3
N_ROWS = 256
VMEM_LIMIT = 60 * 1024 * 1024

BF16 = jnp.bfloat16
F32 = jnp.float32


def _dot(a, b):
    return jnp.dot(a, b, preferred_element_type=F32)


def _rms_norm(x, g):
    return x * lax.rsqrt(jnp.mean(x * x, axis=-1, keepdims=True) + EPS) * g


def _layer_norm(x, g, b):
    mu = jnp.mean(x, axis=-1, keepdims=True)
    xc = x - mu
    var = jnp.mean(xc * xc, axis=-1, keepdims=True)
    return xc * lax.rsqrt(var + EPS) * g + b


def _ada_kernel(c_ref, w_ref, b_ref, o_ref):
    s = jax.nn.silu(c_ref[...]).astype(BF16)
    o_ref[...] = _dot(s, w_ref[...].astype(BF16)) + b_ref[...]


def _ada_call(c_all, w_ada, b_ada):
    depth = w_ada.shape[0]
    rows = c_all.shape[0]
    n_out = w_ada.shape[2]
    return pl.pallas_call(
        _ada_kernel,
        grid=(depth, n_out // ADA_TN),
        in_specs=[
            pl.BlockSpec((rows, D_MODEL), lambda l, j: (0, 0)),
            pl.BlockSpec((None, D_MODEL, ADA_TN), lambda l, j: (l, 0, j)),
            pl.BlockSpec((None, 1, ADA_TN), lambda l, j: (l, 0, j)),
        ],
        out_specs=pl.BlockSpec((None, rows, ADA_TN), lambda l, j: (l, 0, j)),
        out_shape=jax.ShapeDtypeStruct((depth, rows, n_out), F32),
        compiler_params=pltpu.CompilerParams(
            dimension_semantics=("arbitrary", "arbitrary"), vmem_limit_bytes=VMEM_LIMIT),
        name="ada_mod",
    )(c_all, w_ada, b_ada.reshape(depth, 1, n_out))


def _layer_kernel(x_ref, modm_ref, modf_ref, xs_ref, mods_ref, cache_ref, g_mix_ref, w_pool_ref,
                  ps_ref, ln_g_ref, ln_b_ref, w_s_ref, bst_ref, g_ffn_ref, g_fin_ref,
                  w_in_hbm, w_out_hbm, w_gate_hbm, w_up_hbm, w_down_hbm,
                  o_ref, hist_ref, os_ref, hists_ref, vs_ref,
                  w_in_v, w_out_v, w_gate_v, w_up_v, w_down_v, stage_ref, dma_sem,
                  pbuf_ref, ybuf_ref, x1_ref, h2_ref, pbufs_ref, x1s_ref, h2s_ref,
                  *, layer, final, tiles_per_seq, n_tiles):
    t = pl.program_id(0)
    i = jnp.minimum(t, n_tiles - 1) % tiles_per_seq
    tm = x_ref.shape[0]
    ns, ls, _ = xs_ref.shape
    ms = ns * ls

    def load_weights():
        wide_slots = [stage_ref.at[s] for s in range(W_SLOTS)]
        narrow_slots = [x1_ref.at[pl.ds(s * N_ROWS, N_ROWS)] for s in range(tm // N_ROWS)]
        narrow_slots += [x1s_ref.at[pl.ds(s * N_ROWS, N_ROWS)] for s in range(ms // N_ROWS)]
        rings = []
        sem0 = 0
        for slots, rows, pairs in (
                (wide_slots, W_ROWS, ((w_in_hbm, w_in_v), (w_gate_hbm, w_gate_v), (w_up_hbm, w_up_v))),
                (narrow_slots, N_ROWS, ((w_out_hbm, w_out_v), (w_down_hbm, w_down_v)))):
            chunks = [(hbm, vmem, r0) for hbm, vmem in pairs for r0 in range(0, vmem.shape[0], rows)]
            rings.append((slots, rows, chunks, sem0))
            sem0 += len(slots)

        def copy(ring, c):
            slots, rows, chunks, sem0 = ring
            hbm, vmem, r0 = chunks[c]
            s = c % len(slots)
            return pltpu.make_async_copy(hbm.at[layer, pl.ds(r0, rows)],
                                         slots[s].at[pl.ds(0, rows), pl.ds(0, vmem.shape[1])],
                                         dma_sem.at[sem0 + s])

        def finish(ring, c):
            slots, rows, chunks, _ = ring
            _, vmem, r0 = chunks[c]
            copy(ring, c).wait()
            vmem[pl.ds(r0, rows), :] = slots[c % len(slots)][:, pl.ds(0, vmem.shape[1])].astype(BF16)
            if c + len(slots) < len(chunks):
                copy(ring, c + len(slots)).start()

        for ring in rings:
            for c in range(min(len(ring[0]), len(ring[2]))):
                copy(ring, c).start()
        for c in range(max(len(ring[2]) for ring in rings)):
            for ring in rings:
                if c < len(ring[2]):
                    finish(ring, c)

    def sample_rows(k):
        return jnp.broadcast_to(mods_ref[:, k:k + 1, :], (ns, ls, D_MODEL)).reshape(ms, D_MODEL)

    def pool_prompt(p):
        pbuf_ref[HIST:HIST + tm, :] = p
        row = lax.broadcasted_iota(jnp.int32, (HIST, POOL_GROUP), 0) + i * tm
        for g, w in enumerate(POOL_WINDOWS):
            cols = pl.ds(g * POOL_GROUP, POOL_GROUP)
            acc = pbuf_ref[pl.ds(HIST, tm), cols]
            for j in range(1, w):
                acc = acc + pbuf_ref[pl.ds(HIST - j, tm), cols]
            pg = p[:, g * POOL_GROUP:(g + 1) * POOL_GROUP]
            cnt = jnp.minimum(row + 1, w).astype(F32)
            d = jnp.concatenate([acc[:HIST] / cnt, acc[HIST:] * (1.0 / w)], axis=0) - pg
            y = (_dot(d.astype(BF16), w_pool_ref[g].astype(BF16))
                 * ps_ref[:, g * POOL_GROUP:(g + 1) * POOL_GROUP])
            ybuf_ref[:, g * POOL_GROUP:(g + 1) * POOL_GROUP] = y.astype(BF16)
        last = p[tm - HIST:, :]
        pbuf_ref[0:HIST, :] = last
        hist_ref[...] = last

    def pool_sample(p):
        p3 = p.reshape(ns, ls, POOL_WIDTH)
        pbufs_ref[:, 0:HIST, :] = cache_ref[...]
        pbufs_ref[:, HIST:HIST + ls, :] = p3
        hists_ref[...] = p3
        for g, w in enumerate(POOL_WINDOWS):
            cols = pl.ds(g * POOL_GROUP, POOL_GROUP)
            acc = pbufs_ref[:, pl.ds(HIST, ls), cols]
            for j in range(1, w):
                acc = acc + pbufs_ref[:, pl.ds(HIST - j, ls), cols]
            d = (acc * (1.0 / w)).reshape(ms, POOL_GROUP) - p[:, g * POOL_GROUP:(g + 1) * POOL_GROUP]
            y = (_dot(d.astype(BF16), w_pool_ref[g].astype(BF16))
                 * ps_ref[:, g * POOL_GROUP:(g + 1) * POOL_GROUP])
            ybuf_ref[0:ms, g * POOL_GROUP:(g + 1) * POOL_GROUP] = y.astype(BF16)

    def gmlp_prompt(u, v):
        r_idx = lax.broadcasted_iota(jnp.int32, (GMLP_LEN, GMLP_LEN), 0) // CHUNK
        c_idx = lax.broadcasted_iota(jnp.int32, (GMLP_LEN, GMLP_LEN), 1) // CHUNK
        vb = v.astype(BF16)
        for hd in range(GMLP_HEADS):
            wm = jnp.where(c_idx <= r_idx, w_s_ref[hd], 0.0).astype(BF16)
            hc = slice(hd * GMLP_HEAD, (hd + 1) * GMLP_HEAD)
            bias = jnp.broadcast_to(bst_ref[:, hd:hd + 1], (GMLP_LEN, GMLP_HEAD))
            for c in range(tm // GMLP_LEN):
                rc = slice(c * GMLP_LEN, (c + 1) * GMLP_LEN)
                zc = _dot(wm, vb[rc, hc]) + bias
                ybuf_ref[rc, POOL_WIDTH + hd * GMLP_HEAD:POOL_WIDTH + (hd + 1) * GMLP_HEAD] = (
                    u[rc, hc] * zc).astype(BF16)

    def gmlp_sample(u, v):
        vs_ref[...] = v.reshape(ns, ls, GMLP_WIDTH)
        vb = v.astype(BF16)
        r = lax.broadcasted_iota(jnp.int32, (ms, GMLP_LEN), 0)
        k = lax.broadcasted_iota(jnp.int32, (ms, GMLP_LEN), 1)
        pick_rows = (k == r % ls).astype(BF16)
        k = lax.broadcasted_iota(jnp.int32, (GMLP_LEN, ms), 0)
        c = lax.broadcasted_iota(jnp.int32, (GMLP_LEN, ms), 1)
        pick_cols = (k == c % ls).astype(BF16)
        same_stream = (lax.broadcasted_iota(jnp.int32, (ms, ms), 0) // ls
                       == lax.broadcasted_iota(jnp.int32, (ms, ms), 1) // ls)
        for hd in range(GMLP_HEADS):
            hc = slice(hd * GMLP_HEAD, (hd + 1) * GMLP_HEAD)
            rep = _dot(_dot(pick_rows, w_s_ref[hd].astype(BF16)).astype(BF16), pick_cols)
            bd = jnp.where(same_stream, rep, 0.0).astype(BF16)
            bias = jnp.broadcast_to(bst_ref[0:ls, hd:hd + 1], (ls, GMLP_HEAD))
            zc = _dot(bd, vb[:, hc]).reshape(ns, ls, GMLP_HEAD) + bias
            ybuf_ref[0:ms, POOL_WIDTH + hd * GMLP_HEAD:POOL_WIDTH + (hd + 1) * GMLP_HEAD] = (
                u[:, hc].reshape(ns, ls, GMLP_HEAD) * zc).reshape(ms, GMLP_HEAD).astype(BF16)

    def stage(mixer, ffn):
        if mixer == "prompt":
            x, mod = x_ref[...], (lambda k: modm_ref[k:k + 1, :])
        elif mixer == "sample":
            x, mod = xs_ref[...].reshape(ms, D_MODEL), sample_rows
        if mixer:
            h = (_rms_norm(x, g_mix_ref[...]) * (1.0 + mod(1)) + mod(0)).astype(BF16)
        if ffn:
            h2_prev = h2_ref[...] if ffn == "prompt" else h2s_ref[...]
            gate = _dot(h2_prev, w_gate_v[...])
        if mixer:
            z = _dot(h, w_in_v[...])
        if ffn:
            up = _dot(h2_prev, w_up_v[...])
        if mixer:
            p = z[:, :POOL_WIDTH]
            u = jax.nn.gelu(z[:, POOL_WIDTH:POOL_WIDTH + GMLP_WIDTH])
            v = _layer_norm(jax.nn.gelu(z[:, POOL_WIDTH + GMLP_WIDTH:]), ln_g_ref[...], ln_b_ref[...])
            if mixer == "prompt":
                pool_prompt(p)
                gmlp_prompt(u, v)
            else:
                pool_sample(p)
                gmlp_sample(u, v)
        if ffn:
            a = (jax.nn.silu(gate) * up).astype(BF16)
        if mixer:
            mix = _dot(ybuf_ref[0:x.shape[0], :], w_out_v[...])
        if ffn:
            f = _dot(a, w_down_v[...])
        if mixer:
            x1 = x + mod(2) * mix
        if ffn:
            if ffn == "prompt":
                out = x1_ref[...] + modf_ref[5:6, :] * f
            else:
                out = x1s_ref[...] + sample_rows(5) * f
            if final:
                out = _rms_norm(out, g_fin_ref[...])
            if ffn == "prompt":
                o_ref[...] = out
            else:
                os_ref[...] = out.reshape(ns, ls, D_MODEL)
        if mixer:
            h2 = (_rms_norm(x1, g_ffn_ref[...]) * (1.0 + mod(4)) + mod(3)).astype(BF16)
            if mixer == "prompt":
                x1_ref[...] = x1
                h2_ref[...] = h2
            else:
                x1s_ref[...] = x1
                h2s_ref[...] = h2

    @pl.when(i == 0)
    def _():
        pbuf_ref[0:HIST, :] = jnp.zeros((HIST, POOL_WIDTH), F32)

    @pl.when(t == 0)
    def _():
        load_weights()
        stage("prompt", None)

    @pl.when(jnp.logical_and(t > 0, t < n_tiles))
    def _():
        stage("prompt", "prompt")

    @pl.when(t == n_tiles)
    def _():
        stage("sample", "prompt")

    @pl.when(t == n_tiles + 1)
    def _():
        stage(None, "sample")


def _resident(shape, index_map):
    return pl.BlockSpec(shape, index_map, pipeline_mode=pl.Buffered(1))


def _layer_call(layer, final, xp, xs, mod4, cache_pad, small, weights):
    g_mix, w_pool, ps, ln_g, ln_b, w_s, bs_t, g_ffn, g_fin = small
    batch, seq, _ = xp.shape
    ns, ls, _ = xs.shape
    ms = ns * ls
    nt = seq // TM
    n_tiles = batch * nt
    l = layer
    mix_tile = lambda t: jnp.minimum(t, n_tiles - 1)
    ffn_tile = lambda t: jnp.clip(t - 1, 0, n_tiles - 1)
    vec = lambda n: _resident((None, 1, n), lambda t: (l, 0, 0))
    in_specs = [
        pl.BlockSpec((None, TM, D_MODEL), lambda t: (mix_tile(t) // nt, mix_tile(t) % nt, 0)),
        pl.BlockSpec((None, None, N_MOD, D_MODEL), lambda t: (l, ns + mix_tile(t) // nt, 0, 0)),
        pl.BlockSpec((None, None, N_MOD, D_MODEL), lambda t: (l, ns + ffn_tile(t) // nt, 0, 0)),
        _resident((ns, ls, D_MODEL), lambda t: (0, 0, 0)),
        _resident((None, ns, N_MOD, D_MODEL), lambda t: (l, 0, 0, 0)),
        _resident((None, ns, HIST, POOL_WIDTH), lambda t: (l, 0, 0, 0)),
        vec(D_MODEL),
        _resident((None, len(POOL_WINDOWS), POOL_GROUP, POOL_GROUP), lambda t: (l, 0, 0, 0)),
        vec(POOL_WIDTH), vec(GMLP_WIDTH), vec(GMLP_WIDTH),
        _resident((None, GMLP_HEADS, GMLP_LEN, GMLP_LEN), lambda t: (l, 0, 0, 0)),
        _resident((None, GMLP_LEN, GMLP_HEADS), lambda t: (l, 0, 0)),
        vec(D_MODEL),
        _resident((1, D_MODEL), lambda t: (0, 0)),
    ] + [pl.BlockSpec(memory_space=pl.ANY)] * 5
    out_specs = [
        pl.BlockSpec((None, TM, D_MODEL), lambda t: (ffn_tile(t) // nt, ffn_tile(t) % nt, 0)),
        pl.BlockSpec((None, HIST, POOL_WIDTH), lambda t: (mix_tile(t) // nt, 0, 0)),
        pl.BlockSpec((ns, ls, D_MODEL), lambda t: (0, 0, 0)),
        pl.BlockSpec((ns, ls, POOL_WIDTH), lambda t: (0, 0, 0)),
        pl.BlockSpec((ns, ls, GMLP_WIDTH), lambda t: (0, 0, 0)),
    ]
    out_shape = [
        jax.ShapeDtypeStruct(xp.shape, F32),
        jax.ShapeDtypeStruct((batch, HIST, POOL_WIDTH), F32),
        jax.ShapeDtypeStruct(xs.shape, F32),
        jax.ShapeDtypeStruct((ns, ls, POOL_WIDTH), F32),
        jax.ShapeDtypeStruct((ns, ls, GMLP_WIDTH), F32),
    ]
    scratch_shapes = [
        pltpu.VMEM((D_MODEL, D_IN), BF16),
        pltpu.VMEM((D_MODEL, D_MODEL), BF16),
        pltpu.VMEM((D_MODEL, D_FF), BF16),
        pltpu.VMEM((D_MODEL, D_FF), BF16),
        pltpu.VMEM((D_FF, D_MODEL), BF16),
        pltpu.VMEM((W_SLOTS, W_ROWS, D_FF), F32),
        pltpu.SemaphoreType.DMA((W_SLOTS + (TM + ms) // N_ROWS,)),
        pltpu.VMEM((HIST + TM, POOL_WIDTH), F32),
        pltpu.VMEM((TM, D_MODEL), BF16),
        pltpu.VMEM((TM, D_MODEL), F32),
        pltpu.VMEM((TM, D_MODEL), BF16),
        pltpu.VMEM((ns, HIST + ls, POOL_WIDTH), F32),
        pltpu.VMEM((ms, D_MODEL), F32),
        pltpu.VMEM((ms, D_MODEL), BF16),
    ]
    return pl.pallas_call(
        functools.partial(_layer_kernel, layer=layer, final=final, tiles_per_seq=nt, n_tiles=n_tiles),
        grid=(n_tiles + 2,),
        in_specs=in_specs,
        out_specs=out_specs,
        out_shape=out_shape,
        scratch_shapes=scratch_shapes,
        compiler_params=pltpu.CompilerParams(
            dimension_semantics=("arbitrary",), vmem_limit_bytes=VMEM_LIMIT),
        name=f"layer{layer}",
    )(xp, mod4, mod4, xs, mod4, cache_pad, g_mix, w_pool, ps, ln_g, ln_b, w_s, bs_t, g_ffn, g_fin,
      *weights)


def kernel(x_prompt, x_sample, c_prompt, c_sample, cache_pool, w_ada, b_ada, g_mix, w_in, w_pool,
           pool_scale, ln_g, ln_b, w_s, b_s, w_out, g_ffn, w_gate, w_up, w_down, g_final):
    depth = w_ada.shape[0]
    ns, ls, _ = x_sample.shape
    ms = ns * ls
    assert ls == HIST and ms <= TM and x_prompt.shape[1] % TM == 0 and TM % GMLP_LEN == 0

    c_all = jnp.concatenate([c_sample, c_prompt], axis=0)
    pad = (-c_all.shape[0]) % 8
    c_all = jnp.pad(c_all, ((0, pad), (0, 0)))
    mod4 = _ada_call(c_all, w_ada, b_ada).reshape(depth, c_all.shape[0], N_MOD, D_MODEL)

    vec = lambda a: a.reshape(depth, 1, -1)
    g_fin = g_final.reshape(1, D_MODEL)
    bs_t = jnp.swapaxes(b_s, 1, 2)
    cache_pad = jnp.pad(cache_pool, ((0, 0), (0, 0), (HIST - cache_pool.shape[2], 0), (0, 0)))

    small = (vec(g_mix), w_pool, vec(pool_scale), vec(ln_g), vec(ln_b), w_s, bs_t, vec(g_ffn), g_fin)
    weights = (w_in, w_out, w_gate, w_up, w_down)

    xp, xs = x_prompt, x_sample
    pool_p, pool_s, v_s = [], [], []
    for l in range(depth):
        xp, hp, xs, hs, vs = _layer_call(l, l == depth - 1, xp, xs, mod4, cache_pad, small, weights)
        pool_p.append(hp[:, 1:, :])
        pool_s.append(hs[:, 1:, :])
        v_s.append(vs)
    return (xp, xs, jnp.stack(pool_p), jnp.stack(pool_s), jnp.stack(v_s))
```

```python
import functools

import jax
import jax.numpy as jnp
from jax import lax
from jax.experimental import pallas as pl
from jax.experimental.pallas import tpu as pltpu

D_MODEL = 1024
POOL_WIDTH = 512
POOL_WINDOWS = (2, 4, 8, 16)
POOL_GROUP = 128
HIST = 16
GMLP_WIDTH = 512
GMLP_HEADS = 4
GMLP_HEAD = 128
GMLP_LEN = 128
CHUNK = 64
D_IN = 1536
D_FF = 2816
N_MOD = 6
EPS = 1e-6

TM = 512
ADA_TN = 2048
W_ROWS = 128
W_SLOTS = 3
N_ROWS = 256
CAST_ROWS = 32
VMEM_LIMIT = 60 * 1024 * 1024

BF16 = jnp.bfloat16
F32 = jnp.float32


def _dot(a, b):
    return jnp.dot(a, b, preferred_element_type=F32)


def _rms_norm(x, g):
    return x * lax.rsqrt(jnp.mean(x * x, axis=-1, keepdims=True) + EPS) * g


def _layer_norm(x, g, b):
    mu = jnp.mean(x, axis=-1, keepdims=True)
    xc = x - mu
    var = jnp.mean(xc * xc, axis=-1, keepdims=True)
    return xc * lax.rsqrt(var + EPS) * g + b


def _ada_kernel(c_ref, w_ref, b_ref, o_ref):
    s = jax.nn.silu(c_ref[...]).astype(BF16)
    o_ref[...] = _dot(s, w_ref[...].astype(BF16)) + b_ref[...]


def _ada_call(c_all, w_ada, b_ada):
    depth = w_ada.shape[0]
    rows = c_all.shape[0]
    n_out = w_ada.shape[2]
    return pl.pallas_call(
        _ada_kernel,
        grid=(depth, n_out // ADA_TN),
        in_specs=[
            pl.BlockSpec((rows, D_MODEL), lambda l, j: (0, 0)),
            pl.BlockSpec((None, D_MODEL, ADA_TN), lambda l, j: (l, 0, j)),
            pl.BlockSpec((None, 1, ADA_TN), lambda l, j: (l, 0, j)),
        ],
        out_specs=pl.BlockSpec((None, rows, ADA_TN), lambda l, j: (l, 0, j)),
        out_shape=jax.ShapeDtypeStruct((depth, rows, n_out), F32),
        compiler_params=pltpu.CompilerParams(
            dimension_semantics=("arbitrary", "arbitrary"), vmem_limit_bytes=VMEM_LIMIT),
        name="ada_mod",
    )(c_all, w_ada, b_ada.reshape(depth, 1, n_out))


def _layer_kernel(x_ref, modm_ref, modf_ref, xs_ref, mods_ref, cache_ref, g_mix_ref, w_pool_ref,
                  ps_ref, ln_g_ref, ln_b_ref, w_s_ref, bst_ref, g_ffn_ref, g_fin_ref,
                  w_in_hbm, w_out_hbm, w_gate_hbm, w_up_hbm, w_down_hbm,
                  o_ref, hist_ref, os_ref, hists_ref, vs_ref,
                  w_in_v, w_out_v, w_gate_v, w_up_v, w_down_v, stage_ref, stagen_ref, dma_sem,
                  pbuf_ref, ybuf_ref, x1_ref, h2_ref, pbufs_ref,
                  *, layer, final, tiles_per_seq, n_tiles):
    t = pl.program_id(0)
    i = jnp.minimum(t, n_tiles - 1) % tiles_per_seq
    tm = x_ref.shape[0]
    ns, ls, _ = xs_ref.shape
    ms = ns * ls

    def load_weights():
        wide_slots = [stage_ref.at[s] for s in range(W_SLOTS)]
        narrow_slots = [stagen_ref] + [x1_ref.at[pl.ds(s * N_ROWS, N_ROWS)] for s in range(tm // N_ROWS)]
        rings = []
        sem0 = 0
        for slots, rows, pairs in (
                (wide_slots, W_ROWS, ((w_in_hbm, w_in_v), (w_gate_hbm, w_gate_v), (w_up_hbm, w_up_v))),
                (narrow_slots, N_ROWS, ((w_out_hbm, w_out_v), (w_down_hbm, w_down_v)))):
            chunks = [(hbm, vmem, r0) for hbm, vmem in pairs for r0 in range(0, vmem.shape[0], rows)]
            rings.append((slots, rows, chunks, sem0))
            sem0 += len(slots)

        def copy(ring, c):
            slots, rows, chunks, sem0 = ring
            hbm, vmem, r0 = chunks[c]
            s = c % len(slots)
            return pltpu.make_async_copy(hbm.at[layer, pl.ds(r0, rows)],
                                         slots[s].at[pl.ds(0, rows), pl.ds(0, vmem.shape[1])],
                                         dma_sem.at[sem0 + s])

        def finish(ring, c):
            slots, rows, chunks, _ = ring
            _, vmem, r0 = chunks[c]
            slot = slots[c % len(slots)]
            copy(ring, c).wait()

            def cast_rows(g, carry):
                r = pl.multiple_of(g * CAST_ROWS, CAST_ROWS)
                vmem[pl.ds(r0 + r, CAST_ROWS), :] = (
                    slot[pl.ds(r, CAST_ROWS), pl.ds(0, vmem.shape[1])].astype(BF16))
                return carry

            lax.fori_loop(0, rows // CAST_ROWS, cast_rows, 0)
            if c + len(slots) < len(chunks):
                copy(ring, c + len(slots)).start()

        for ring in rings:
            for c in range(min(len(ring[0]), len(ring[2]))):
                copy(ring, c).start()
        for c in range(max(len(ring[2]) for ring in rings)):
            for ring in rings:
                if c < len(ring[2]):
                    finish(ring, c)

    def sample_rows(k):
        return jnp.broadcast_to(mods_ref[:, k:k + 1, :], (ns, ls, D_MODEL)).reshape(ms, D_MODEL)

    def pool_prompt(p):
        pbuf_ref[HIST:HIST + tm, :] = p
        row = lax.broadcasted_iota(jnp.int32, (HIST, POOL_GROUP), 0) + i * tm
        for g, w in enumerate(POOL_WINDOWS):
            cols = pl.ds(g * POOL_GROUP, POOL_GROUP)
            acc = pbuf_ref[pl.ds(HIST, tm), cols]
            for j in range(1, w):
                acc = acc + pbuf_ref[pl.ds(HIST - j, tm), cols]
            pg = p[:, g * POOL_GROUP:(g + 1) * POOL_GROUP]
            cnt = jnp.minimum(row + 1, w).astype(F32)
            d = jnp.concatenate([acc[:HIST] / cnt, acc[HIST:] * (1.0 / w)], axis=0) - pg
            y = (_dot(d.astype(BF16), w_pool_ref[g].astype(BF16))
                 * ps_ref[:, g * POOL_GROUP:(g + 1) * POOL_GROUP])
            ybuf_ref[:, g * POOL_GROUP:(g + 1) * POOL_GROUP] = y.astype(BF16)
        last = p[tm - HIST:, :]
        pbuf_ref[0:HIST, :] = last
        hist_ref[...] = last

    def pool_sample(p):
        p3 = p.reshape(ns, ls, POOL_WIDTH)
        pbufs_ref[:, 0:HIST, :] = cache_ref[...]
        pbufs_ref[:, HIST:HIST + ls, :] = p3
        hists_ref[...] = p3
        for g, w in enumerate(POOL_WINDOWS):
            cols = pl.ds(g * POOL_GROUP, POOL_GROUP)
            acc = pbufs_ref[:, pl.ds(HIST, ls), cols]
            for j in range(1, w):
                acc = acc + pbufs_ref[:, pl.ds(HIST - j, ls), cols]
            d = (acc * (1.0 / w)).reshape(ms, POOL_GROUP) - p[:, g * POOL_GROUP:(g + 1) * POOL_GROUP]
            y = (_dot(d.astype(BF16), w_pool_ref[g].astype(BF16))
                 * ps_ref[:, g * POOL_GROUP:(g + 1) * POOL_GROUP])
            ybuf_ref[0:ms, g * POOL_GROUP:(g + 1) * POOL_GROUP] = y.astype(BF16)

    def gmlp_prompt(u, v):
        r_idx = lax.broadcasted_iota(jnp.int32, (GMLP_LEN, GMLP_LEN), 0) // CHUNK
        c_idx = lax.broadcasted_iota(jnp.int32, (GMLP_LEN, GMLP_LEN), 1) // CHUNK
        vb = v.astype(BF16)
        for hd in range(GMLP_HEADS):
            wm = jnp.where(c_idx <= r_idx, w_s_ref[hd], 0.0).astype(BF16)
            hc = slice(hd * GMLP_HEAD, (hd + 1) * GMLP_HEAD)
            bias = jnp.broadcast_to(bst_ref[:, hd:hd + 1], (GMLP_LEN, GMLP_HEAD))
            for c in range(tm // GMLP_LEN):
                rc = slice(c * GMLP_LEN, (c + 1) * GMLP_LEN)
                zc = _dot(wm, vb[rc, hc]) + bias
                ybuf_ref[rc, POOL_WIDTH + hd * GMLP_HEAD:POOL_WIDTH + (hd + 1) * GMLP_HEAD] = (
                    u[rc, hc] * zc).astype(BF16)

    def gmlp_sample(u, v):
        vs_ref[...] = v.reshape(ns, ls, GMLP_WIDTH)
        vb = v.astype(BF16)
        r = lax.broadcasted_iota(jnp.int32, (ms, GMLP_LEN), 0)
        k = lax.broadcasted_iota(jnp.int32, (ms, GMLP_LEN), 1)
        pick_rows = (k == r % ls).astype(BF16)
        k = lax.broadcasted_iota(jnp.int32, (GMLP_LEN, ms), 0)
        c = lax.broadcasted_iota(jnp.int32, (GMLP_LEN, ms), 1)
        pick_cols = (k == c % ls).astype(BF16)
        same_stream = (lax.broadcasted_iota(jnp.int32, (ms, ms), 0) // ls
                       == lax.broadcasted_iota(jnp.int32, (ms, ms), 1) // ls)
        for hd in range(GMLP_HEADS):
            hc = slice(hd * GMLP_HEAD, (hd + 1) * GMLP_HEAD)
            rep = _dot(_dot(pick_rows, w_s_ref[hd].astype(BF16)).astype(BF16), pick_cols)
            bd = jnp.where(same_stream, rep, 0.0).astype(BF16)
            bias = jnp.broadcast_to(bst_ref[0:ls, hd:hd + 1], (ls, GMLP_HEAD))
            zc = _dot(bd, vb[:, hc]).reshape(ns, ls, GMLP_HEAD) + bias
            ybuf_ref[0:ms, POOL_WIDTH + hd * GMLP_HEAD:POOL_WIDTH + (hd + 1) * GMLP_HEAD] = (
                u[:, hc].reshape(ns, ls, GMLP_HEAD) * zc).reshape(ms, GMLP_HEAD).astype(BF16)

    def stage(kind, lagged_ffn):
        if kind == "prompt":
            x, mod = x_ref[...], (lambda k: modm_ref[k:k + 1, :])
        else:
            x, mod = xs_ref[...].reshape(ms, D_MODEL), sample_rows
        h = (_rms_norm(x, g_mix_ref[...]) * (1.0 + mod(1)) + mod(0)).astype(BF16)
        if lagged_ffn:
            h2_prev = h2_ref[...]
            gate = _dot(h2_prev, w_gate_v[...])
        z = _dot(h, w_in_v[...])
        if lagged_ffn:
            up = _dot(h2_prev, w_up_v[...])
        p = z[:, :POOL_WIDTH]
        u = jax.nn.gelu(z[:, POOL_WIDTH:POOL_WIDTH + GMLP_WIDTH])
        v = _layer_norm(jax.nn.gelu(z[:, POOL_WIDTH + GMLP_WIDTH:]), ln_g_ref[...], ln_b_ref[...])
        if kind == "prompt":
            pool_prompt(p)
            gmlp_prompt(u, v)
        else:
            pool_sample(p)
            gmlp_sample(u, v)
        if lagged_ffn:
            a = (jax.nn.silu(gate) * up).astype(BF16)
        mix = _dot(ybuf_ref[0:x.shape[0], :], w_out_v[...])
        if lagged_ffn:
            f = _dot(a, w_down_v[...])
        x1 = x + mod(2) * mix
        if lagged_ffn:
            out = x1_ref[...] + modf_ref[5:6, :] * f
            if final:
                out = _rms_norm(out, g_fin_ref[...])
            o_ref[...] = out
        h2 = (_rms_norm(x1, g_ffn_ref[...]) * (1.0 + mod(4)) + mod(3)).astype(BF16)
        if kind == "prompt":
            x1_ref[...] = x1
            h2_ref[...] = h2
        else:
            a = (jax.nn.silu(_dot(h2, w_gate_v[...])) * _dot(h2, w_up_v[...])).astype(BF16)
            out = x1 + mod(5) * _dot(a, w_down_v[...])
            if final:
                out = _rms_norm(out, g_fin_ref[...])
            os_ref[...] = out.reshape(ns, ls, D_MODEL)

    @pl.when(i == 0)
    def _():
        pbuf_ref[0:HIST, :] = jnp.zeros((HIST, POOL_WIDTH), F32)

    @pl.when(t == 0)
    def _():
        load_weights()
        stage("prompt", False)

    @pl.when(jnp.logical_and(t > 0, t <= n_tiles))
    def _():
        stage("prompt", True)

    @pl.when(t == n_tiles + 1)
    def _():
        stage("sample", False)


def _resident(shape, index_map):
    return pl.BlockSpec(shape, index_map, pipeline_mode=pl.Buffered(1))


def _layer_call(layer, final, xp, xs, mod4, cache_pad, small, weights):
    g_mix, w_pool, ps, ln_g, ln_b, w_s, bs_t, g_ffn, g_fin = small
    batch, seq, _ = xp.shape
    ns, ls, _ = xs.shape
    ms = ns * ls
    nt = seq // TM
    n_tiles = batch * nt
    l = layer
    mix_tile = lambda t: jnp.minimum(t, n_tiles - 1)
    ffn_tile = lambda t: jnp.clip(t - 1, 0, n_tiles - 1)
    vec = lambda n: _resident((None, 1, n), lambda t: (l, 0, 0))
    in_specs = [
        pl.BlockSpec((None, TM, D_MODEL), lambda t: (mix_tile(t) // nt, mix_tile(t) % nt, 0)),
        pl.BlockSpec((None, None, N_MOD, D_MODEL), lambda t: (l, ns + mix_tile(t) // nt, 0, 0)),
        pl.BlockSpec((None, None, N_MOD, D_MODEL), lambda t: (l, ns + ffn_tile(t) // nt, 0, 0)),
        _resident((ns, ls, D_MODEL), lambda t: (0, 0, 0)),
        _resident((None, ns, N_MOD, D_MODEL), lambda t: (l, 0, 0, 0)),
        _resident((None, ns, HIST, POOL_WIDTH), lambda t: (l, 0, 0, 0)),
        vec(D_MODEL),
        _resident((None, len(POOL_WINDOWS), POOL_GROUP, POOL_GROUP), lambda t: (l, 0, 0, 0)),
        vec(POOL_WIDTH), vec(GMLP_WIDTH), vec(GMLP_WIDTH),
        _resident((None, GMLP_HEADS, GMLP_LEN, GMLP_LEN), lambda t: (l, 0, 0, 0)),
        _resident((None, GMLP_LEN, GMLP_HEADS), lambda t: (l, 0, 0)),
        vec(D_MODEL),
        _resident((1, D_MODEL), lambda t: (0, 0)),
    ] + [pl.BlockSpec(memory_space=pl.ANY)] * 5
    out_specs = [
        pl.BlockSpec((None, TM, D_MODEL), lambda t: (ffn_tile(t) // nt, ffn_tile(t) % nt, 0)),
        pl.BlockSpec((None, HIST, POOL_WIDTH), lambda t: (mix_tile(t) // nt, 0, 0)),
        pl.BlockSpec((ns, ls, D_MODEL), lambda t: (0, 0, 0)),
        pl.BlockSpec((ns, ls, POOL_WIDTH), lambda t: (0, 0, 0)),
        pl.BlockSpec((ns, ls, GMLP_WIDTH), lambda t: (0, 0, 0)),
    ]
    out_shape = [
        jax.ShapeDtypeStruct(xp.shape, F32),
        jax.ShapeDtypeStruct((batch, HIST, POOL_WIDTH), F32),
        jax.ShapeDtypeStruct(xs.shape, F32),
        jax.ShapeDtypeStruct((ns, ls, POOL_WIDTH), F32),
        jax.ShapeDtypeStruct((ns, ls, GMLP_WIDTH), F32),
    ]
    scratch_shapes = [
        pltpu.VMEM((D_MODEL, D_IN), BF16),
        pltpu.VMEM((D_MODEL, D_MODEL), BF16),
        pltpu.VMEM((D_MODEL, D_FF), BF16),
        pltpu.VMEM((D_MODEL, D_FF), BF16),
        pltpu.VMEM((D_FF, D_MODEL), BF16),
        pltpu.VMEM((W_SLOTS, W_ROWS, D_FF), F32),
        pltpu.VMEM((N_ROWS, D_MODEL), F32),
        pltpu.SemaphoreType.DMA((W_SLOTS + 1 + TM // N_ROWS,)),
        pltpu.VMEM((HIST + TM, POOL_WIDTH), F32),
        pltpu.VMEM((TM, D_MODEL), BF16),
        pltpu.VMEM((TM, D_MODEL), F32),
        pltpu.VMEM((TM, D_MODEL), BF16),
        pltpu.VMEM((ns, HIST + ls, POOL_WIDTH), F32),
    ]
    return pl.pallas_call(
        functools.partial(_layer_kernel, layer=layer, final=final, tiles_per_seq=nt, n_tiles=n_tiles),
        grid=(n_tiles + 2,),
        in_specs=in_specs,
        out_specs=out_specs,
        out_shape=out_shape,
        scratch_shapes=scratch_shapes,
        compiler_params=pltpu.CompilerParams(
            dimension_semantics=("arbitrary",), vmem_limit_bytes=VMEM_LIMIT),
        name=f"layer{layer}",
    )(xp, mod4, mod4, xs, mod4, cache_pad, g_mix, w_pool, ps, ln_g, ln_b, w_s, bs_t, g_ffn, g_fin,
      *weights)


def kernel(x_prompt, x_sample, c_prompt, c_sample, cache_pool, w_ada, b_ada, g_mix, w_in, w_pool,
           pool_scale, ln_g, ln_b, w_s, b_s, w_out, g_ffn, w_gate, w_up, w_down, g_final):
    depth = w_ada.shape[0]
    ns, ls, _ = x_sample.shape
    ms = ns * ls
    assert ls == HIST and ms <= TM and x_prompt.shape[1] % TM == 0 and TM % GMLP_LEN == 0

    c_all = jnp.concatenate([c_sample, c_prompt], axis=0)
    pad = (-c_all.shape[0]) % 8
    c_all = jnp.pad(c_all, ((0, pad), (0, 0)))
    mod4 = _ada_call(c_all, w_ada, b_ada).reshape(depth, c_all.shape[0], N_MOD, D_MODEL)

    vec = lambda a: a.reshape(depth, 1, -1)
    g_fin = g_final.reshape(1, D_MODEL)
    bs_t = jnp.swapaxes(b_s, 1, 2)
    cache_pad = jnp.pad(cache_pool, ((0, 0), (0, 0), (HIST - cache_pool.shape[2], 0), (0, 0)))

    small = (vec(g_mix), w_pool, vec(pool_scale), vec(ln_g), vec(ln_b), w_s, bs_t, vec(g_ffn), g_fin)
    weights = (w_in, w_out, w_gate, w_up, w_down)

    xp, xs = x_prompt, x_sample
    pool_p, pool_s, v_s = [], [], []
    for l in range(depth):
        xp, hp, xs, hs, vs = _layer_call(l, l == depth - 1, xp, xs, mod4, cache_pad, small, weights)
        pool_p.append(hp[:, 1:, :])
        pool_s.append(hs[:, 1:, :])
        v_s.append(vs)
    return (xp, xs, jnp.stack(pool_p), jnp.stack(pool_s), jnp.stack(v_s))
```
